```python
import jax, jax.numpy as jnp
from jax import lax
import numpy as np

D_MODEL = 1024
BATCH = 16
SEQ = 4096
DEPTH = 4
DEC_BATCH = 2
DEC_SEQ = 16384
PAST_LEN = 128

N_EVEN = (DEPTH + 1) // 2
N_ODD = DEPTH // 2
D_FF = 2816
EPS = 1e-6
H_A = 4
DK_A = 64
DV_A = 128
ALPHA_RANK = 16
GATE_NORM = 16.0
GLA_CHUNK = 64
H_B = 8
Q_RANK = 256
KV_RANK = 128
NOPE_B = 64
ROPE_B = 32
V_B = 64
ROPE_BASE = 10000.0
Q_BLOCK = 128
H_C = 4
CG_C = 128
H_D = 4
DG_D = 128
SGU_CHUNK = 128
EV_SIZES = (H_A * DK_A, H_A * DK_A, H_A * DV_A, H_A * DV_A, 2 * ALPHA_RANK, Q_RANK, KV_RANK, ROPE_B)
EV_IN = sum(EV_SIZES)
D_MIX_EV = H_A * DV_A + H_B * V_B
OD_SIZES = (H_C * CG_C, 2 * H_D * DG_D)
OD_IN = sum(OD_SIZES)
D_MIX_OD = H_C * CG_C + H_D * DG_D

kernel_name = 'hybrid_bidir_gla_mla_fnet_sgu_encoder'


def _split_cols(z, sizes):
    out, off = [], 0
    for s in sizes:
        out.append(z[..., off:off + s])
        off += s
    return out


def _rmsnorm(x, g):
    xf = x.astype(jnp.float32)
    y = xf * lax.rsqrt(jnp.mean(xf * xf, axis=-1, keepdims=True) + EPS)
    return (y * g.astype(jnp.float32)).astype(x.dtype)


def _rope(x, cos, sin):
    xf = x.astype(jnp.float32)
    x1, x2 = jnp.split(xf, 2, axis=-1)
    return jnp.concatenate([x1 * cos - x2 * sin, x1 * sin + x2 * cos], axis=-1).astype(x.dtype)


def _swiglu(h, w13, w2):
    a, b = jnp.split(h @ w13, 2, axis=-1)
    return (jax.nn.silu(a) * b) @ w2


def _sublayer(x, f, m, g_pre, g_post, w_res):
    shift, scale, gate = m[:, 0], m[:, 1], m[:, 2]
    h = _rmsnorm(x, g_pre) * (1.0 + scale[:, None, :]) + shift[:, None, :]
    return x + (w_res * (1.0 + gate))[:, None, :] * _rmsnorm(f(h), g_post)


def _gla_chunked(q, k, v, lg, strict):
    B, H, S, dk = q.shape
    dv = v.shape[-1]
    n = S // GLA_CHUNK
    r = lambda t: t.reshape(B, H, n, GLA_CHUNK, t.shape[-1])
    q, k, v, lg = r(q), r(k), r(v), r(lg)
    b = jnp.cumsum(lg, axis=-2)
    bq = b - lg if strict else b
    qt = q * jnp.exp(bq)
    kt = k * jnp.exp(-b)
    kd = k * jnp.exp(b[..., -1:, :] - b)
    mask = jnp.tril(jnp.ones((GLA_CHUNK, GLA_CHUNK), dtype=bool), -1 if strict else 0)
    a = jnp.where(mask, jnp.einsum('bhnid,bhnjd->bhnij', qt, kt), 0.0)
    intra = jnp.einsum('bhnij,bhnjv->bhniv', a, v)
    decay = jnp.exp(b[..., -1, :])
    kv = jnp.einsum('bhnjd,bhnjv->bhndv', kd, v)

    def step(state, inp):
        dec, kvc = inp
        return dec[..., None] * state + kvc, state

    _, s_prev = lax.scan(step, jnp.zeros((B, H, dk, dv), jnp.float32),
                         (jnp.moveaxis(decay, 2, 0), jnp.moveaxis(kv, 2, 0)))
    s_prev = jnp.moveaxis(s_prev, 0, 2)
    inter = jnp.einsum('bhnid,bhndv->bhniv', qt, s_prev)
    return (intra + inter).reshape(B, H, S, dv)


def _gla_mixer(q, k, v, g, a_lr, w_alpha, b_alpha, gain):
    B, S, _ = q.shape
    f32 = jnp.float32

    def heads(t, d):
        return t.astype(f32).reshape(B, S, H_A, d).transpose(0, 2, 1, 3)

    qh = heads(q, DK_A) * DK_A ** -0.5
    kh, vh = heads(k, DK_A), heads(v, DV_A)
    a_f, a_b = jnp.split(a_lr.astype(f32), 2, axis=-1)
    lg_f = heads(jax.nn.log_sigmoid(a_f @ w_alpha[0].astype(f32) + b_alpha[0].astype(f32)) / GATE_NORM, DK_A)
    lg_b = heads(jax.nn.log_sigmoid(a_b @ w_alpha[1].astype(f32) + b_alpha[1].astype(f32)) / GATE_NORM, DK_A)
    flip = lambda t: jnp.flip(t, axis=2)
    o = _gla_chunked(qh, kh, vh, lg_f, False) + flip(_gla_chunked(flip(qh), flip(kh), flip(vh), flip(lg_b), True))
    o = _rmsnorm(o.transpose(0, 2, 1, 3), gain).reshape(B, S, H_A * DV_A)
    return (o * jax.nn.silu(g.astype(f32))).astype(q.dtype)


def _mla_mixer(cq, ckv, kr, q_norm, w_q_b, kv_norm, w_kv_b, cos, sin):
    B, S, _ = cq.shape
    q = (_rmsnorm(cq, q_norm) @ w_q_b).reshape(B, S, H_B, NOPE_B + ROPE_B)
    q_nope = q[..., :NOPE_B]
    q_rope = _rope(q[..., NOPE_B:], cos[None, :, None, :], sin[None, :, None, :])
    kv = (_rmsnorm(ckv, kv_norm) @ w_kv_b).reshape(B, S, H_B, NOPE_B + V_B)
    k_nope, v = kv[..., :NOPE_B], kv[..., NOPE_B:]
    k_rope = _rope(kr, cos[None], sin[None])
    scale = (NOPE_B + ROPE_B) ** -0.5
    nq = S // Q_BLOCK
    blocks = lambda t: jnp.swapaxes(t.reshape(B, nq, Q_BLOCK, *t.shape[2:]), 0, 1)

    def attend(qs):
        qn, qr = qs
        s = (jnp.einsum('bqhd,bkhd->bhqk', qn, k_nope, preferred_element_type=jnp.float32)
             + jnp.einsum('bqhr,bkr->bhqk', qr, k_rope, preferred_element_type=jnp.float32)) * scale
        p = jax.nn.softmax(s, axis=-1)
        return jnp.einsum('bhqk,bkhd->bqhd', p.astype(v.dtype), v)

    o = lax.map(attend, (blocks(q_nope), blocks(q_rope)))
    return jnp.swapaxes(o, 0, 1).reshape(B, S, H_B * V_B)


def _even_mixer(h, w_in, w_out, w_alpha, b_alpha, gla_gain, q_norm, w_q_b, kv_norm, w_kv_b, cos, sin):
    q, k, v, g, a_lr, cq, ckv, kr = _split_cols(h @ w_in, EV_SIZES)
    oa = _gla_mixer(q, k, v, g, a_lr, w_alpha, b_alpha, gla_gain)
    ob = _mla_mixer(cq, ckv, kr, q_norm, w_q_b, kv_norm, w_kv_b, cos, sin)
    return jnp.concatenate([oa, ob], axis=-1) @ w_out


def _odd_mixer(h, w_in, w_out, v_norm, w_s, b_s):
    B, S, _ = h.shape
    zc, zd = _split_cols(h @ w_in, OD_SIZES)
    fc = jnp.fft.fft2(zc.astype(jnp.float32).reshape(B, S, H_C, CG_C), axes=(1, 3), norm='ortho').real
    fc = fc.reshape(B, S, H_C * CG_C).astype(h.dtype)
    u, v = jnp.split(jax.nn.gelu(zd, approximate=False), 2, axis=-1)
    v = _rmsnorm(v, v_norm).reshape(B, S // SGU_CHUNK, SGU_CHUNK, H_D, DG_D)
    sv = jnp.einsum('hij,bnjhc->bnihc', w_s, v) + jnp.transpose(b_s)[None, None, :, :, None]
    od = u * sv.reshape(B, S, H_D * DG_D)
    return jnp.concatenate([fc, od], axis=-1) @ w_out


def setup_inputs(seed: int = 0) -> dict:
    key = jax.random.key(seed)
    ks = jax.random.split(key, 24)
    nrm = lambda k, shape, s: s * jax.random.normal(k, shape, jnp.float32)
    D = D_MODEL
    return {
        'x_prompt': nrm(ks[0], (BATCH, SEQ, D), 1.0),
        'x_sample': nrm(ks[1], (DEC_BATCH, DEC_SEQ, D), 1.0),
        'c_prompt': nrm(ks[2], (BATCH, D), 1.0),
        'c_sample': nrm(ks[3], (DEC_BATCH, D), 1.0),
        'ada_w': nrm(ks[4], (DEPTH, D, 9 * D), 0.5 * D ** -0.5),
        'ada_b': nrm(ks[5], (DEPTH, 9 * D), 0.01),
        'norm_pre': 1.0 + nrm(ks[6], (DEPTH, 3, D), 0.05),
        'norm_post': 1.0 + nrm(ks[7], (DEPTH, 3, D), 0.05),
        'ffn_w13': nrm(ks[8], (DEPTH, 2, D, 2 * D_FF), D ** -0.5),
        'ffn_w2': nrm(ks[9], (DEPTH, 2, D_FF, D), D_FF ** -0.5),
        'ev_w_in': nrm(ks[10], (N_EVEN, D, EV_IN), D ** -0.5),
        'ev_w_out': nrm(ks[11], (N_EVEN, D_MIX_EV, D), D_MIX_EV ** -0.5),
        'gla_w_alpha': nrm(ks[12], (N_EVEN, 2, ALPHA_RANK, H_A * DK_A), ALPHA_RANK ** -0.5),
        'gla_b_alpha': nrm(ks[13], (N_EVEN, 2, H_A * DK_A), 0.1),
        'gla_norm': 1.0 + nrm(ks[14], (N_EVEN, DV_A), 0.05),
        'mla_q_norm': 1.0 + nrm(ks[15], (N_EVEN, Q_RANK), 0.05),
        'mla_w_q_b': nrm(ks[16], (N_EVEN, Q_RANK, H_B * (NOPE_B + ROPE_B)), Q_RANK ** -0.5),
        'mla_kv_norm': 1.0 + nrm(ks[17], (N_EVEN, KV_RANK), 0.05),
        'mla_w_kv_b': nrm(ks[18], (N_EVEN, KV_RANK, H_B * (NOPE_B + V_B)), KV_RANK ** -0.5),
        'od_w_in': nrm(ks[19], (N_ODD, D, OD_IN), D ** -0.5),
        'od_w_out': nrm(ks[20], (N_ODD, D_MIX_OD, D), D_MIX_OD ** -0.5),
        'sgu_norm': 1.0 + nrm(ks[21], (N_ODD, H_D * DG_D), 0.05),
        'sgu_w_s': nrm(ks[22], (N_ODD, H_D, SGU_CHUNK, SGU_CHUNK), 0.5 * SGU_CHUNK ** -0.5),
        'sgu_b': 1.0 + nrm(ks[23], (N_ODD, H_D, SGU_CHUNK), 0.1),
    }


def reference(x_prompt, x_sample, c_prompt, c_sample, ada_w, ada_b, norm_pre, norm_post, ffn_w13, ffn_w2,
              ev_w_in, ev_w_out, gla_w_alpha, gla_b_alpha, gla_norm, mla_q_norm, mla_w_q_b, mla_kv_norm,
              mla_w_kv_b, od_w_in, od_w_out, sgu_norm, sgu_w_s, sgu_b):
    def run(x, c):
        B, S, _ = x.shape
        half = ROPE_B // 2
        inv = ROPE_BASE ** (-jnp.arange(half, dtype=jnp.float32) / half)
        ang = jnp.arange(S, dtype=jnp.float32)[:, None] * inv[None, :]
        cos, sin = jnp.cos(ang), jnp.sin(ang)
        cc = jax.nn.silu(c)
        for l in range(DEPTH):
            m = (cc @ ada_w[l] + ada_b[l]).reshape(B, 3, 3, D_MODEL)
            x = _sublayer(x, lambda h: _swiglu(h, ffn_w13[l, 0], ffn_w2[l, 0]), m[:, 0],
                          norm_pre[l, 0], norm_post[l, 0], 0.5)
            i = l // 2
            if l % 2 == 0:
                mix = lambda h: _even_mixer(h, ev_w_in[i], ev_w_out[i], gla_w_alpha[i], gla_b_alpha[i], gla_norm[i],
                                            mla_q_norm[i], mla_w_q_b[i], mla_kv_norm[i], mla_w_kv_b[i], cos, sin)
            else:
                mix = lambda h: _odd_mixer(h, od_w_in[i], od_w_out[i], sgu_norm[i], sgu_w_s[i], sgu_b[i])
            x = _sublayer(x, mix, m[:, 1], norm_pre[l, 1], norm_post[l, 1], 1.0)
            x = _sublayer(x, lambda h: _swiglu(h, ffn_w13[l, 1], ffn_w2[l, 1]), m[:, 2],
                          norm_pre[l, 2], norm_post[l, 2], 0.5)
        return x

    y_prompt = run(x_prompt, c_prompt)
    y_sample = run(x_sample, c_sample)
    return (y_prompt, y_sample)
```

```python
import functools
import math

import jax
import jax.numpy as jnp
import numpy as np
from jax import lax
from jax.experimental import pallas as pl
from jax.experimental.pallas import tpu as pltpu

F32 = jnp.float32
BF16 = jnp.bfloat16

D_MODEL = 1024
DEPTH = 4
D_FF = 2816
EPS = 1e-6
H_A, DK_A, DV_A = 4, 64, 128
ALPHA_RANK = 16
GATE_NORM = 16.0
GLA_CHUNK = 64
H_B, Q_RANK, KV_RANK, NOPE_B, ROPE_B, V_B = 8, 256, 128, 64, 32, 64
ROPE_BASE = 10000.0
H_C, CG_C = 4, 128
H_D, DG_D = 4, 128
SGU_CHUNK = 128
HEAD_SLOT = 128

VMEM_LIMIT_BYTES = 56 * 1024 * 1024
LANES = 128

FFN_TM = 512
FFN_CHUNKS = (1024, 1024, 768)
PROJ_TM = 512
GLA_TS = 256
ATT_TQ = 512
ATT_TK = 1024


def _cparams(sem):
    return pltpu.CompilerParams(dimension_semantics=sem, vmem_limit_bytes=VMEM_LIMIT_BYTES)


def _resident(shape):
    nd = len(shape)
    return pl.BlockSpec(shape, lambda *_: (0,) * nd, pipeline_mode=pl.Buffered(1))


def _rms(x, g):
    return x * lax.rsqrt(jnp.mean(x * x, axis=-1, keepdims=True) + EPS) * g


def _modulated(x, mod_ref, g_pre):
    shift = mod_ref[0:1, :]
    scale = mod_ref[1:2, :]
    return _rms(x, g_pre) * (1.0 + scale) + shift


def _residual(x, y, mod_ref, g_post, w_res):
    gate = mod_ref[2:3, :]
    return x + (w_res * (1.0 + gate)) * _rms(y, g_post)


def _silu(a):
    return a * jax.nn.sigmoid(a)


def _mod_spec(layer, sub, b_off):
    return pl.BlockSpec((None, None, None, 3, D_MODEL), lambda b, i: (layer, b_off + b, sub, 0, 0))


def _tok_spec(tm, width):
    return pl.BlockSpec((None, tm, width), lambda b, i: (b, i, 0))


def _ada_kernel(c_ref, w_ref, b_ref, o_ref):
    cc = _silu(c_ref[...]).astype(BF16)
    o_ref[...] = jnp.dot(cc, w_ref[...].astype(BF16), preferred_element_type=F32) + b_ref[...]


def _ada(c_all, ada_w, ada_b):
    bt = c_all.shape[0]
    n = ada_w.shape[-1]
    tn = n // 8
    return pl.pallas_call(
        _ada_kernel,
        grid=(DEPTH, n // tn),
        in_specs=[
            pl.BlockSpec((bt, D_MODEL), lambda l, j: (0, 0)),
            pl.BlockSpec((None, D_MODEL, tn), lambda l, j: (l, 0, j)),
            pl.BlockSpec((None, 1, tn), lambda l, j: (l, 0, j)),
        ],
        out_specs=pl.BlockSpec((None, bt, tn), lambda l, j: (l, 0, j)),
        out_shape=jax.ShapeDtypeStruct((DEPTH, bt, n), F32),
        compiler_params=_cparams(("arbitrary", "arbitrary")),
        name="ada_mod",
    )(c_all, ada_w, ada_b.reshape(DEPTH, 1, n))


def _ffn_kernel(x_ref, mod_ref, gpre_ref, gpost_ref, w13_ref, w2_ref, o_ref):
    x = x_ref[...]
    h = _modulated(x, mod_ref, gpre_ref[...]).astype(BF16)
    acc = None
    off = 0
    for ck in FFN_CHUNKS:
        a = jnp.dot(h, w13_ref[:, off:off + ck], preferred_element_type=F32)
        b = jnp.dot(h, w13_ref[:, D_FF + off:D_FF + off + ck], preferred_element_type=F32)
        act = (_silu(a) * b).astype(BF16)
        part = jnp.dot(act, w2_ref[off:off + ck, :], preferred_element_type=F32)
        acc = part if acc is None else acc + part
        off += ck
    o_ref[...] = _residual(x, acc, mod_ref, gpost_ref[...], 0.5)


def _ffn(x, mod5, layer, sub, which, b_off, g_pre, g_post, w13, w2):
    B, S, D = x.shape
    tm = min(FFN_TM, S)
    return pl.pallas_call(
        _ffn_kernel,
        grid=(B, S // tm),
        in_specs=[
            _tok_spec(tm, D),
            _mod_spec(layer, sub, b_off),
            _resident((1, D)),
            _resident((1, D)),
            pl.BlockSpec((None, None, D, 2 * D_FF), lambda b, i: (layer, which, 0, 0), pipeline_mode=pl.Buffered(1)),
            pl.BlockSpec((None, None, D_FF, D), lambda b, i: (layer, which, 0, 0), pipeline_mode=pl.Buffered(1)),
        ],
        out_specs=_tok_spec(tm, D),
        out_shape=jax.ShapeDtypeStruct((B, S, D), F32),
        compiler_params=_cparams(("arbitrary", "arbitrary")),
        name="ffn",
    )(x, mod5, g_pre, g_post, w13, w2)


EV_Q, EV_K, EV_V, EV_G, EV_CQ, EV_CKV, EV_A, EV_B = 0, 256, 512, 1024, 1536, 1792, 1920, 2048
EV_WIDTH = 2176
MLA_W = H_B * HEAD_SLOT


def _even_in_kernel(x_ref, mod_ref, gpre_ref, win_ref, walpha_ref, balpha_ref, qn_ref, wq_ref, kvn_ref,
                    wk_ref, wv_ref, e_ref, cosq_ref, sinq_ref, cosk_ref, sink_ref,
                    qg_o, kg_o, vg_o, g_o, lg_o, q_o, k_o, v_o):
    x = x_ref[...]
    h = _modulated(x, mod_ref, gpre_ref[...]).astype(BF16)
    z = jnp.dot(h, win_ref[...], preferred_element_type=F32)
    qg_o[...] = z[:, EV_Q:EV_K].astype(BF16)
    kg_o[...] = z[:, EV_K:EV_V].astype(BF16)
    vg_o[...] = z[:, EV_V:EV_G].astype(BF16)
    g_o[...] = z[:, EV_G:EV_CQ]
    cq = z[:, EV_CQ:EV_CKV]
    ckv = z[:, EV_CKV:EV_A]
    blk_a = z[:, EV_A:EV_B]
    blk_b = z[:, EV_B:EV_WIDTH]
    al = jnp.dot(blk_a.astype(BF16), walpha_ref[...], preferred_element_type=F32) + balpha_ref[...]
    lg_o[...] = (jnp.minimum(al, 0.0) - jnp.log1p(jnp.exp(-jnp.abs(al)))) * (1.0 / GATE_NORM)
    cqn = _rms(cq, qn_ref[...]).astype(BF16)
    q2 = jnp.dot(cqn, wq_ref[...], preferred_element_type=F32)
    cosq = cosq_ref[...]
    sinq = sinq_ref[...]
    scale = (NOPE_B + ROPE_B) ** -0.5
    for hh in range(H_B):
        lo = hh * HEAD_SLOT
        qs = q2[:, lo:lo + HEAD_SLOT] * cosq + q2[:, MLA_W + lo:MLA_W + lo + HEAD_SLOT] * sinq
        q_o[:, lo:lo + HEAD_SLOT] = (qs * scale).astype(BF16)
    ckvn = _rms(ckv, kvn_ref[...]).astype(BF16)
    krr = (blk_a * cosk_ref[...] + blk_b * sink_ref[...]).astype(BF16)
    k_o[...] = (jnp.dot(ckvn, wk_ref[...], preferred_element_type=F32)
                + jnp.dot(krr, e_ref[...], preferred_element_type=F32)).astype(BF16)
    v_o[...] = jnp.dot(ckvn, wv_ref[...], preferred_element_type=F32).astype(BF16)


def _even_in(x, mod5, layer, b_off, g_pre, wp, tabs):
    B, S, D = x.shape
    tm = min(PROJ_TM, S)
    pos_spec = pl.BlockSpec((tm, LANES), lambda b, i: (i, 0))
    outs = [(256, BF16), (256, BF16), (512, BF16), (512, F32), (512, F32),
            (MLA_W, BF16), (MLA_W, BF16), (MLA_W, BF16)]
    return pl.pallas_call(
        _even_in_kernel,
        grid=(B, S // tm),
        in_specs=[
            _tok_spec(tm, D),
            _mod_spec(layer, 1, b_off),
            _resident((1, D)),
            _resident((D, EV_WIDTH)),
            _resident((LANES, 512)),
            _resident((1, 512)),
            _resident((1, Q_RANK)),
            _resident((Q_RANK, 2 * MLA_W)),
            _resident((1, KV_RANK)),
            _resident((KV_RANK, MLA_W)),
            _resident((KV_RANK, MLA_W)),
            _resident((LANES, MLA_W)),
            pos_spec, pos_spec, pos_spec, pos_spec,
        ],
        out_specs=[_tok_spec(tm, w) for w, _ in outs],
        out_shape=[jax.ShapeDtypeStruct((B, S, w), dt) for w, dt in outs],
        compiler_params=_cparams(("arbitrary", "arbitrary")),
        name="even_in",
    )(x, mod5, g_pre, wp["w_in"], wp["w_alpha"], wp["b_alpha"], wp["q_norm"], wp["w_q"], wp["kv_norm"],
      wp["w_k"], wp["w_v"], wp["e_rope"], tabs["cosq"], tabs["sinq"], tabs["cosk"], tabs["sink"])


def _dot_hilo(m, x):
    hi = x.astype(BF16)
    lo = (x - hi.astype(F32)).astype(BF16)
    return (jnp.dot(m, hi, preferred_element_type=F32) + jnp.dot(m, lo, preferred_element_type=F32))


def _gla_direction(q, k, lg, v, cum_ref, same_ref, amask, s_ref, reverse):
    ts = GLA_TS
    nchunk = ts // GLA_CHUNK
    b = _dot_hilo(cum_ref[...], lg)
    tot = _dot_hilo(same_ref[...], lg)
    bq = (b - lg) if reverse else b
    qt = q * jnp.exp(bq)
    kt = (k * jnp.exp(-b)).astype(BF16)
    kd = k * jnp.exp(tot - b)
    lane = lax.broadcasted_iota(jnp.int32, (ts, 2 * DK_A), 1)
    outs = []
    for hh in range(2):
        qh = jnp.where((lane >= hh * DK_A) & (lane < (hh + 1) * DK_A), qt, 0.0).astype(BF16)
        a = lax.dot_general(qh, kt, (((1,), (1,)), ((), ())), preferred_element_type=F32)
        a = jnp.where(amask, a, 0.0).astype(BF16)
        outs.append(jnp.dot(a, v[:, hh * DV_A:(hh + 1) * DV_A], preferred_element_type=F32))
    intra = jnp.concatenate(outs, axis=-1)
    kdt = kd.T.astype(BF16)
    tott = tot.T
    row = lax.broadcasted_iota(jnp.int32, (ts, 2 * DV_A), 0)
    srow = lax.broadcasted_iota(jnp.int32, (2 * DK_A, 2 * DV_A), 0)
    scol = lax.broadcasted_iota(jnp.int32, (2 * DK_A, 2 * DV_A), 1)
    own = (srow >= DK_A) == (scol >= DV_A)
    qtb = qt.astype(BF16)
    state = s_ref[...]
    inter = [None] * nchunk
    order = range(nchunk - 1, -1, -1) if reverse else range(nchunk)
    for n in order:
        r0 = n * GLA_CHUNK
        inter[n] = jnp.dot(qtb[r0:r0 + GLA_CHUNK, :], state.astype(BF16), preferred_element_type=F32)
        vn = jnp.where((row >= r0) & (row < r0 + GLA_CHUNK), v, jnp.zeros_like(v))
        kv = jnp.dot(kdt, vn, preferred_element_type=F32)
        dec = jnp.exp(tott[:, r0:r0 + 1])
        state = jnp.where(own, dec * state + kv, 0.0)
    s_ref[...] = state
    return intra + jnp.concatenate(inter, axis=0)


def _gla_kernel(qf_ref, kf_ref, lgf_ref, vf_ref, qb_ref, kb_ref, lgb_ref, vb_ref,
                cumf_ref, cumb_ref, same_ref, of_ref, ob_ref, sf_ref, sb_ref):
    @pl.when(pl.program_id(2) == 0)
    def _():
        sf_ref[...] = jnp.zeros_like(sf_ref)
        sb_ref[...] = jnp.zeros_like(sb_ref)

    ts = GLA_TS
    r = lax.broadcasted_iota(jnp.int32, (ts, ts), 0)
    c = lax.broadcasted_iota(jnp.int32, (ts, ts), 1)
    same = (r >> 6) == (c >> 6)
    scale = DK_A ** -0.5
    of_ref[...] = _gla_direction(qf_ref[...].astype(F32) * scale, kf_ref[...].astype(F32), lgf_ref[...],
                                 vf_ref[...], cumf_ref, same_ref, same & (c <= r), sf_ref, False)
    ob_ref[...] = _gla_direction(qb_ref[...].astype(F32) * scale, kb_ref[...].astype(F32), lgb_ref[...],
                                 vb_ref[...], cumb_ref, same_ref, same & (c > r), sb_ref, True)


def _gla(qg, kg, vg, lg, consts):
    B, S, _ = qg.shape
    ts = GLA_TS
    nt = S // ts
    fwd = lambda w, off: pl.BlockSpec((None, ts, w), lambda b, p, t: (b, t, p + off))
    bwd = lambda w, off: pl.BlockSpec((None, ts, w), lambda b, p, t: (b, nt - 1 - t, p + off))
    npair = H_A // 2
    return pl.pallas_call(
        _gla_kernel,
        grid=(B, npair, nt),
        in_specs=[
            fwd(128, 0), fwd(128, 0), fwd(128, 0), fwd(256, 0),
            bwd(128, 0), bwd(128, 0), bwd(128, npair), bwd(256, 0),
            _resident((ts, ts)), _resident((ts, ts)), _resident((ts, ts)),
        ],
        out_specs=[fwd(256, 0), bwd(256, 0)],
        out_shape=[jax.ShapeDtypeStruct((B, S, H_A * DV_A), F32)] * 2,
        scratch_shapes=[pltpu.VMEM((2 * DK_A, 2 * DV_A), F32)] * 2,
        compiler_params=_cparams(("arbitrary", "arbitrary", "arbitrary")),
        name="gla",
    )(qg, kg, lg, vg, qg, kg, lg, vg, consts["cum_f"], consts["cum_b"], consts["same"])


def _attn_kernel(q_ref, k_ref, v_ref, o_ref, *, tk, nk):
    q = q_ref[...]
    tq = q.shape[0]

    def body(j, carry):
        m, l, acc = carry
        start = pl.multiple_of(j * tk, tk)
        kk = k_ref[pl.ds(start, tk), :]
        vv = v_ref[pl.ds(start, tk), :]
        s = lax.dot_general(q, kk, (((1,), (1,)), ((), ())), preferred_element_type=F32)
        m_new = jnp.maximum(m, jnp.max(s, axis=-1, keepdims=True))
        alpha = jnp.exp(m - m_new)
        p = jnp.exp(s - m_new)
        l = alpha * l + jnp.sum(p, axis=-1, keepdims=True)
        acc = alpha * acc + jnp.dot(p.astype(BF16), vv, preferred_element_type=F32)
        return m_new, l, acc

    init = (jnp.full((tq, 1), -jnp.inf, F32), jnp.zeros((tq, 1), F32), jnp.zeros((tq, HEAD_SLOT), F32))
    m, l, acc = lax.fori_loop(0, nk, body, init)
    o_ref[...] = (acc / l).astype(BF16)


def _attn(q, k, v):
    B, S, _ = q.shape
    tq = min(ATT_TQ, S)
    tk = min(ATT_TK, S)
    kv_spec = pl.BlockSpec((None, S, HEAD_SLOT), lambda b, h, i: (b, 0, h))
    return pl.pallas_call(
        functools.partial(_attn_kernel, tk=tk, nk=S // tk),
        grid=(B, H_B, S // tq),
        in_specs=[pl.BlockSpec((None, tq, HEAD_SLOT), lambda b, h, i: (b, i, h)), kv_spec, kv_spec],
        out_specs=pl.BlockSpec((None, tq, HEAD_SLOT), lambda b, h, i: (b, i, h)),
        out_shape=jax.ShapeDtypeStruct((B, S, MLA_W), BF16),
        compiler_params=_cparams(("arbitrary", "arbitrary", "arbitrary")),
        name="mla_attn",
    )(q, k, v)


def _even_out_kernel(x_ref, mod_ref, gpost_ref, of_ref, ob_ref, g_ref, gain_ref, ao_ref, wa_ref, wb_ref, o_ref):
    o = of_ref[...] + ob_ref[...]
    g = g_ref[...]
    gain = gain_ref[...]
    parts = []
    for hh in range(H_A):
        sl = slice(hh * DV_A, (hh + 1) * DV_A)
        parts.append(_rms(o[:, sl], gain) * _silu(g[:, sl]))
    oa = jnp.concatenate(parts, axis=-1).astype(BF16)
    mix = (jnp.dot(oa, wa_ref[...], preferred_element_type=F32)
           + jnp.dot(ao_ref[...], wb_ref[...], preferred_element_type=F32))
    o_ref[...] = _residual(x_ref[...], mix, mod_ref, gpost_ref[...], 1.0)


def _even_out(x, mod5, layer, b_off, g_post, o_f, o_b, g, gain, ao, wa, wb):
    B, S, D = x.shape
    tm = min(PROJ_TM, S)
    return pl.pallas_call(
        _even_out_kernel,
        grid=(B, S // tm),
        in_specs=[
            _tok_spec(tm, D), _mod_spec(layer, 1, b_off), _resident((1, D)),
            _tok_spec(tm, 512), _tok_spec(tm, 512), _tok_spec(tm, 512), _resident((1, DV_A)),
            _tok_spec(tm, MLA_W), _resident((H_A * DV_A, D)), _resident((MLA_W, D)),
        ],
        out_specs=_tok_spec(tm, D),
        out_shape=jax.ShapeDtypeStruct((B, S, D), F32),
        compiler_params=_cparams(("arbitrary", "arbitrary")),
        name="even_out",
    )(x, mod5, g_post, o_f, o_b, g, gain, ao, wa, wb)


def _odd_in_kernel(x_ref, mod_ref, gpre_ref, win_ref, sgun_ref, ws_ref, bs_ref, zc_o, od_o):
    x = x_ref[...]
    tm = x.shape[0]
    h = _modulated(x, mod_ref, gpre_ref[...]).astype(BF16)
    z = jnp.dot(h, win_ref[...], preferred_element_type=F32)
    wc = H_C * CG_C
    wd = H_D * DG_D
    zc_o[...] = z[:, :wc].astype(BF16)
    zd = z[:, wc:]
    ge = 0.5 * zd * (1.0 + lax.erf(zd * (2.0 ** -0.5)))
    u = ge[:, :wd]
    vn = _rms(ge[:, wd:], sgun_ref[...]).astype(BF16)
    for c in range(tm // SGU_CHUNK):
        rs = slice(c * SGU_CHUNK, (c + 1) * SGU_CHUNK)
        for hh in range(H_D):
            cs = slice(hh * DG_D, (hh + 1) * DG_D)
            sv = jnp.dot(ws_ref[hh], vn[rs, cs], preferred_element_type=F32) + bs_ref[hh]
            od_o[rs, cs] = (u[rs, cs] * sv).astype(BF16)


def _odd_in(x, mod5, layer, b_off, g_pre, w_in, sgu_norm, w_s, b_s):
    B, S, D = x.shape
    tm = min(PROJ_TM, S)
    return pl.pallas_call(
        _odd_in_kernel,
        grid=(B, S // tm),
        in_specs=[
            _tok_spec(tm, D), _mod_spec(layer, 1, b_off), _resident((1, D)),
            _resident((D, 1536)), _resident((1, 512)),
            _resident((H_D, SGU_CHUNK, SGU_CHUNK)), _resident((H_D, SGU_CHUNK, DG_D)),
        ],
        out_specs=[_tok_spec(tm, 512), _tok_spec(tm, 512)],
        out_shape=[jax.ShapeDtypeStruct((B, S, 512), BF16)] * 2,
        compiler_params=_cparams(("arbitrary", "arbitrary")),
        name="odd_in",
    )(x, mod5, g_pre, w_in, sgu_norm, w_s, b_s)


def _fnet1_kernel(f_ref, z_ref, a_ref):
    a_ref[...] = jnp.dot(f_ref[...], z_ref[...], preferred_element_type=F32).astype(BF16)


def _fnet2_kernel(a_ref, cs_ref, g_ref, o_ref, *, kt, n2, norm):
    wc = H_C * CG_C
    for j in range(kt):
        a2 = a_ref[:, j].reshape(2 * n2, wc)
        br, bi = [], []
        for gg in range(H_C):
            p = jnp.dot(a2[:, gg * CG_C:(gg + 1) * CG_C], cs_ref[...], preferred_element_type=F32)
            top, bot = p[:n2], p[n2:]
            br.append(top[:, :CG_C] + bot[:, CG_C:])
            bi.append(bot[:, :CG_C] - top[:, CG_C:])
        b2 = jnp.concatenate([jnp.concatenate(br, axis=-1), jnp.concatenate(bi, axis=-1)], axis=0).astype(BF16)
        zz = jnp.dot(g_ref[j], b2, preferred_element_type=F32)
        o_ref[:, j * wc:(j + 1) * wc] = (zz * norm).astype(BF16)


def _fnet(zc, consts):
    B, S, wc = zc.shape
    n1, n2 = consts["n1"], consts["n2"]
    tc = min(n2 * wc, 8192)
    a = pl.pallas_call(
        _fnet1_kernel,
        grid=(B, n2 * wc // tc),
        in_specs=[_resident((2 * n1, n1)), pl.BlockSpec((None, n1, tc), lambda b, j: (b, 0, j))],
        out_specs=pl.BlockSpec((None, 2 * n1, tc), lambda b, j: (b, 0, j)),
        out_shape=jax.ShapeDtypeStruct((B, 2 * n1, n2 * wc), BF16),
        compiler_params=_cparams(("arbitrary", "arbitrary")),
        name="fnet_stage1",
    )(consts["f1"], zc.reshape(B, n1, n2 * wc))
    kt = 8
    out = pl.pallas_call(
        functools.partial(_fnet2_kernel, kt=kt, n2=n2, norm=float(1.0 / math.sqrt(S * CG_C))),
        grid=(B, n1 // kt),
        in_specs=[
            pl.BlockSpec((None, 2, kt, n2, wc), lambda b, j: (b, 0, j, 0, 0)),
            _resident((CG_C, 2 * CG_C)),
            pl.BlockSpec((kt, n2, 2 * n2), lambda b, j: (j, 0, 0)),
        ],
        out_specs=pl.BlockSpec((None, n2, kt * wc), lambda b, j: (b, 0, j)),
        out_shape=jax.ShapeDtypeStruct((B, n2, n1 * wc), BF16),
        compiler_params=_cparams(("arbitrary", "arbitrary")),
        name="fnet_stage2",
    )(a.reshape(B, 2, n1, n2, wc), consts["cs"], consts["g"])
    return out.reshape(B, S, wc)


def _odd_out_kernel(x_ref, mod_ref, gpost_ref, fc_ref, od_ref, wa_ref, wb_ref, o_ref):
    mix = (jnp.dot(fc_ref[...], wa_ref[...], preferred_element_type=F32)
           + jnp.dot(od_ref[...], wb_ref[...], preferred_element_type=F32))
    o_ref[...] = _residual(x_ref[...], mix, mod_ref, gpost_ref[...], 1.0)


def _odd_out(x, mod5, layer, b_off, g_post, fc, od, wa, wb):
    B, S, D = x.shape
    tm = min(PROJ_TM, S)
    return pl.pallas_call(
        _odd_out_kernel,
        grid=(B, S // tm),
        in_specs=[
            _tok_spec(tm, D), _mod_spec(layer, 1, b_off), _resident((1, D)),
            _tok_spec(tm, 512), _tok_spec(tm, 512), _resident((512, D)), _resident((512, D)),
        ],
        out_specs=_tok_spec(tm, D),
        out_shape=jax.ShapeDtypeStruct((B, S, D), F32),
        compiler_params=_cparams(("arbitrary", "arbitrary")),
        name="odd_out",
    )(x, mod5, g_post, fc, od, wa, wb)


def _gla_consts():
    ts = GLA_TS
    r = np.arange(ts)[:, None]
    c = np.arange(ts)[None, :]
    same = (r // GLA_CHUNK) == (c // GLA_CHUNK)
    as_bf16 = lambda m: jnp.asarray(m.astype(np.float32), dtype=BF16)
    return {"cum_f": as_bf16(same & (c <= r)), "cum_b": as_bf16(same & (c >= r)), "same": as_bf16(same)}


def _rope_tables(S):
    half = ROPE_B // 2
    inv = ROPE_BASE ** (-jnp.arange(half, dtype=F32) / half)
    ang = jnp.arange(S, dtype=F32)[:, None] * inv[None, :]
    cos, sin = jnp.cos(ang), jnp.sin(ang)
    cc = jnp.concatenate([cos, cos], axis=-1)
    ss = jnp.concatenate([sin, sin], axis=-1)
    z = lambda w: jnp.zeros((S, w), F32)
    return {
        "cosq": jnp.concatenate([jnp.ones((S, NOPE_B), F32), cc, z(HEAD_SLOT - NOPE_B - ROPE_B)], axis=-1),
        "sinq": jnp.concatenate([z(NOPE_B), ss, z(HEAD_SLOT - NOPE_B - ROPE_B)], axis=-1),
        "cosk": jnp.concatenate([cc, z(LANES - ROPE_B)], axis=-1),
        "sink": jnp.concatenate([ss, z(LANES - ROPE_B)], axis=-1),
    }


def _fnet_consts(S):
    n1 = 1 << (int(math.log2(S)) // 2)
    n2 = S // n1
    two_pi = 2.0 * math.pi
    k1 = jnp.arange(n1, dtype=jnp.int32)
    ph1 = ((k1[:, None] * k1[None, :]) % n1).astype(F32) * (two_pi / n1)
    f1 = jnp.concatenate([jnp.cos(ph1), -jnp.sin(ph1)], axis=0).astype(BF16)
    ch = jnp.arange(CG_C, dtype=jnp.int32)
    phc = ((ch[:, None] * ch[None, :]) % CG_C).astype(F32) * (two_pi / CG_C)
    cs = jnp.concatenate([jnp.cos(phc), jnp.sin(phc)], axis=-1).astype(BF16)
    s2 = jnp.arange(n2, dtype=jnp.int32)
    k = k1[:, None, None] + n1 * s2[None, :, None]
    ph = ((k * s2[None, None, :]) % S).astype(F32) * (two_pi / S)
    g = jnp.concatenate([jnp.cos(ph), jnp.sin(ph)], axis=-1).astype(BF16)
    return {"n1": n1, "n2": n2, "f1": f1, "cs": cs, "g": g}


def _even_weights(w_in, w_out, w_alpha, b_alpha, gla_gain, q_norm, w_q_b, kv_norm, w_kv_b):
    D = D_MODEL
    offs = np.cumsum([0, 256, 256, 512, 512, 2 * ALPHA_RANK, Q_RANK, KV_RANK, ROPE_B])
    col = lambda i: w_in[:, offs[i]:offs[i + 1]]
    kr = col(7)
    half = ROPE_B // 2
    kr_rot = jnp.concatenate([-kr[:, half:], kr[:, :half]], axis=-1)
    zc = lambda w: jnp.zeros((D, w), F32)
    win = jnp.concatenate([col(0), col(1), col(2), col(3), col(5), col(6),
                           kr, col(4), zc(64), kr_rot, zc(96)], axis=-1).astype(BF16)
    assert win.shape[1] == EV_WIDTH
    wal = jnp.zeros((LANES, 512), F32)
    wal = wal.at[ROPE_B:ROPE_B + ALPHA_RANK, :256].set(w_alpha[0])
    wal = wal.at[ROPE_B + ALPHA_RANK:ROPE_B + 2 * ALPHA_RANK, 256:].set(w_alpha[1])
    bal = jnp.concatenate([b_alpha[0], b_alpha[1]])[None, :]
    wq = w_q_b.reshape(Q_RANK, H_B, NOPE_B + ROPE_B)
    qn, qr = wq[..., :NOPE_B], wq[..., NOPE_B:]
    pad = HEAD_SLOT - NOPE_B - ROPE_B
    zq = lambda w: jnp.zeros((Q_RANK, H_B, w), F32)
    wq_main = jnp.concatenate([qn, qr, zq(pad)], axis=-1).reshape(Q_RANK, MLA_W)
    wq_rot = jnp.concatenate([zq(NOPE_B), -qr[..., half:], qr[..., :half], zq(pad)], axis=-1).reshape(Q_RANK, MLA_W)
    wq2 = jnp.concatenate([wq_main, wq_rot], axis=-1).astype(BF16)
    wkv = w_kv_b.reshape(KV_RANK, H_B, NOPE_B + V_B)
    zk = lambda w: jnp.zeros((KV_RANK, H_B, w), F32)
    wk = jnp.concatenate([wkv[..., :NOPE_B], zk(HEAD_SLOT - NOPE_B)], axis=-1).reshape(KV_RANK, MLA_W).astype(BF16)
    wv = jnp.concatenate([wkv[..., NOPE_B:], zk(HEAD_SLOT - V_B)], axis=-1).reshape(KV_RANK, MLA_W).astype(BF16)
    e = np.zeros((LANES, H_B, HEAD_SLOT), np.float32)
    for r in range(ROPE_B):
        e[r, :, NOPE_B + r] = 1.0
    e = jnp.asarray(e.reshape(LANES, MLA_W), dtype=BF16)
    wo_a = w_out[:H_A * DV_A].astype(BF16)
    wo_b = jnp.concatenate([w_out[H_A * DV_A:].reshape(H_B, V_B, D),
                            jnp.zeros((H_B, HEAD_SLOT - V_B, D), F32)], axis=1).reshape(MLA_W, D).astype(BF16)
    return {"w_in": win, "w_alpha": wal.astype(BF16), "b_alpha": bal, "q_norm": q_norm[None, :], "w_q": wq2,
            "kv_norm": kv_norm[None, :], "w_k": wk, "w_v": wv, "e_rope": e, "gain": gla_gain[None, :],
            "wo_a": wo_a, "wo_b": wo_b}


def kernel(x_prompt, x_sample, c_prompt, c_sample, ada_w, ada_b, norm_pre, norm_post, ffn_w13, ffn_w2,
           ev_w_in, ev_w_out, gla_w_alpha, gla_b_alpha, gla_norm, mla_q_norm, mla_w_q_b, mla_kv_norm,
           mla_w_kv_b, od_w_in, od_w_out, sgu_norm, sgu_w_s, sgu_b):
    groups = [(x_prompt, 0), (x_sample, x_prompt.shape[0])]
    c_all = jnp.concatenate([c_prompt, c_sample], axis=0)
    mod5 = _ada(c_all, ada_w, ada_b).reshape(DEPTH, c_all.shape[0], 3, 3, D_MODEL)

    w13 = ffn_w13.astype(BF16)
    w2 = ffn_w2.astype(BF16)
    gla_consts = _gla_consts()
    rope = {x.shape[1]: _rope_tables(x.shape[1]) for x, _ in groups}
    fnet = {x.shape[1]: _fnet_consts(x.shape[1]) for x, _ in groups}
    even_w = [_even_weights(ev_w_in[i], ev_w_out[i], gla_w_alpha[i], gla_b_alpha[i], gla_norm[i], mla_q_norm[i],
                            mla_w_q_b[i], mla_kv_norm[i], mla_w_kv_b[i]) for i in range(ev_w_in.shape[0])]
    od_win = od_w_in.astype(BF16)
    od_wout = od_w_out.astype(BF16)
    sgu_ws = sgu_w_s.astype(BF16)
    sgu_bias = jnp.broadcast_to(sgu_b[..., None], sgu_b.shape + (DG_D,))

    outs = []
    for x, b_off in groups:
        S = x.shape[1]
        for l in range(DEPTH):
            pre = lambda s: norm_pre[l, s][None, :]
            post = lambda s: norm_post[l, s][None, :]
            x = _ffn(x, mod5, l, 0, 0, b_off, pre(0), post(0), w13, w2)
            i = l // 2
            if l % 2 == 0:
                wp = even_w[i]
                qg, kg, vg, g, lg, q, k, v = _even_in(x, mod5, l, b_off, pre(1), wp, rope[S])
                o_f, o_b = _gla(qg, kg, vg, lg, gla_consts)
                ao = _attn(q, k, v)
                x = _even_out(x, mod5, l, b_off, post(1), o_f, o_b, g, wp["gain"], ao, wp["wo_a"], wp["wo_b"])
            else:
                zc, od = _odd_in(x, mod5, l, b_off, pre(1), od_win[i], sgu_norm[i][None, :], sgu_ws[i], sgu_bias[i])
                fc = _fnet(zc, fnet[S])
                x = _odd_out(x, mod5, l, b_off, post(1), fc, od, od_wout[i, :512], od_wout[i, 512:])
            x = _ffn(x, mod5, l, 2, 1, b_off, pre(2), post(2), w13, w2)
        outs.append(x)
    return tuple(outs)
```

```python
import functools
import math

import jax
import jax.numpy as jnp
import numpy as np
from jax import lax
from jax.experimental import pallas as pl
from jax.experimental.pallas import tpu as pltpu

F32 = jnp.float32
BF16 = jnp.bfloat16

D_MODEL = 1024
DEPTH = 4
D_FF = 2816
EPS = 1e-6
H_A, DK_A, DV_A = 4, 64, 128
ALPHA_RANK = 16
GATE_NORM = 16.0
GLA_CHUNK = 64
H_B, Q_RANK, KV_RANK, NOPE_B, ROPE_B, V_B = 8, 256, 128, 64, 32, 64
ROPE_BASE = 10000.0
H_C, CG_C = 4, 128
H_D, DG_D = 4, 128
SGU_CHUNK = 128
HEAD_SLOT = 128

VMEM_LIMIT_BYTES = 56 * 1024 * 1024
LANES = 128

FFN_TM = 512
FFN_STREAMS = 2
FFN_CHUNKS = (1024, 1024, 768)
PROJ_TM = 512
GLA_TS = 256
GLA_STEP = 1024
ATT_TQ = 512
ATT_TK = 1024
ATT_HEADS = 2


def _cparams(sem):
    return pltpu.CompilerParams(dimension_semantics=sem, vmem_limit_bytes=VMEM_LIMIT_BYTES)


def _resident(shape):
    nd = len(shape)
    return pl.BlockSpec(shape, lambda *_: (0,) * nd, pipeline_mode=pl.Buffered(1))


def _rms(x, g):
    return x * lax.rsqrt(jnp.mean(x * x, axis=-1, keepdims=True) + EPS) * g


def _modulated(x, mod_ref, g_pre):
    shift = mod_ref[0:1, :]
    scale = mod_ref[1:2, :]
    return _rms(x, g_pre) * (1.0 + scale) + shift


def _residual(x, y, mod_ref, g_post, w_res):
    gate = mod_ref[2:3, :]
    return x + (w_res * (1.0 + gate)) * _rms(y, g_post)


def _silu(a):
    return a * jax.nn.sigmoid(a)


def _mod_spec(layer, sub, b_off):
    return pl.BlockSpec((None, None, None, 3, D_MODEL), lambda b, i: (layer, b_off + b, sub, 0, 0))


def _tok_spec(tm, width):
    return pl.BlockSpec((None, tm, width), lambda b, i: (b, i, 0))


def _ada_kernel(c_ref, w_ref, b_ref, o_ref):
    cc = _silu(c_ref[...]).astype(BF16)
    o_ref[...] = jnp.dot(cc, w_ref[...].astype(BF16), preferred_element_type=F32) + b_ref[...]


def _ada(c_all, ada_w, ada_b):
    bt = c_all.shape[0]
    n = ada_w.shape[-1]
    tn = n // 8
    return pl.pallas_call(
        _ada_kernel,
        grid=(DEPTH, n // tn),
        in_specs=[
            pl.BlockSpec((bt, D_MODEL), lambda l, j: (0, 0)),
            pl.BlockSpec((None, D_MODEL, tn), lambda l, j: (l, 0, j)),
            pl.BlockSpec((None, 1, tn), lambda l, j: (l, 0, j)),
        ],
        out_specs=pl.BlockSpec((None, bt, tn), lambda l, j: (l, 0, j)),
        out_shape=jax.ShapeDtypeStruct((DEPTH, bt, n), F32),
        compiler_params=_cparams(("arbitrary", "arbitrary")),
        name="ada_mod",
    )(c_all, ada_w, ada_b.reshape(DEPTH, 1, n))


def _ffn_rows(x, mod_ref, gpre, gpost, w13_ref, w2_ref):
    h = _modulated(x, mod_ref, gpre).astype(BF16)
    acc = None
    off = 0
    for ck in FFN_CHUNKS:
        a = jnp.dot(h, w13_ref[:, off:off + ck], preferred_element_type=F32)
        b = jnp.dot(h, w13_ref[:, D_FF + off:D_FF + off + ck], preferred_element_type=F32)
        act = (_silu(a) * b).astype(BF16)
        part = jnp.dot(act, w2_ref[off:off + ck, :], preferred_element_type=F32)
        acc = part if acc is None else acc + part
        off += ck
    return _residual(x, acc, mod_ref, gpost, 0.5)


def _ffn_kernel(x_ref, mod_ref, gpre_ref, gpost_ref, w13_ref, w2_ref, o_ref):
    rows = x_ref.shape[0] // FFN_STREAMS
    for r in range(FFN_STREAMS):
        sl = slice(r * rows, (r + 1) * rows)
        o_ref[sl, :] = _ffn_rows(x_ref[sl, :], mod_ref, gpre_ref[...], gpost_ref[...], w13_ref, w2_ref)


def _ffn(x, mod5, layer, sub, which, b_off, g_pre, g_post, w13, w2):
    B, S, D = x.shape
    tm = min(FFN_TM, S)
    return pl.pallas_call(
        _ffn_kernel,
        grid=(B, S // tm),
        in_specs=[
            _tok_spec(tm, D),
            _mod_spec(layer, sub, b_off),
            _resident((1, D)),
            _resident((1, D)),
            pl.BlockSpec((None, None, D, 2 * D_FF), lambda b, i: (layer, which, 0, 0), pipeline_mode=pl.Buffered(1)),
            pl.BlockSpec((None, None, D_FF, D), lambda b, i: (layer, which, 0, 0), pipeline_mode=pl.Buffered(1)),
        ],
        out_specs=_tok_spec(tm, D),
        out_shape=jax.ShapeDtypeStruct((B, S, D), F32),
        compiler_params=_cparams(("arbitrary", "arbitrary")),
        name="ffn",
    )(x, mod5, g_pre, g_post, w13, w2)


EV_Q, EV_K, EV_V, EV_G, EV_CQ, EV_CKV, EV_A, EV_B = 0, 256, 512, 1024, 1536, 1792, 1920, 2048
EV_WIDTH = 2176
MLA_W = H_B * HEAD_SLOT


def _even_in_kernel(x_ref, mod_ref, gpre_ref, win_ref, walpha_ref, balpha_ref, qn_ref, wq_ref, kvn_ref,
                    wk_ref, wv_ref, e_ref, cosq_ref, sinq_ref, cosk_ref, sink_ref,
                    qg_o, kg_o, vg_o, g_o, lg_o, q_o, k_o, v_o):
    x = x_ref[...]
    h = _modulated(x, mod_ref, gpre_ref[...]).astype(BF16)
    z = jnp.dot(h, win_ref[...], preferred_element_type=F32)
    qg_o[...] = z[:, EV_Q:EV_K].astype(BF16)
    kg_o[...] = z[:, EV_K:EV_V].astype(BF16)
    vg_o[...] = z[:, EV_V:EV_G].astype(BF16)
    g_o[...] = z[:, EV_G:EV_CQ]
    cq = z[:, EV_CQ:EV_CKV]
    ckv = z[:, EV_CKV:EV_A]
    blk_a = z[:, EV_A:EV_B]
    blk_b = z[:, EV_B:EV_WIDTH]
    al = jnp.dot(blk_a.astype(BF16), walpha_ref[...], preferred_element_type=F32) + balpha_ref[...]
    lg_o[...] = (jnp.minimum(al, 0.0) - jnp.log1p(jnp.exp(-jnp.abs(al)))) * (1.0 / GATE_NORM)
    cqn = _rms(cq, qn_ref[...]).astype(BF16)
    q2 = jnp.dot(cqn, wq_ref[...], preferred_element_type=F32)
    cosq = cosq_ref[...]
    sinq = sinq_ref[...]
    scale = (NOPE_B + ROPE_B) ** -0.5 * math.log2(math.e)
    for hh in range(H_B):
        lo = hh * HEAD_SLOT
        qs = q2[:, lo:lo + HEAD_SLOT] * cosq + q2[:, MLA_W + lo:MLA_W + lo + HEAD_SLOT] * sinq
        q_o[:, lo:lo + HEAD_SLOT] = (qs * scale).astype(BF16)
    ckvn = _rms(ckv, kvn_ref[...]).astype(BF16)
    krr = (blk_a * cosk_ref[...] + blk_b * sink_ref[...]).astype(BF16)
    k_o[...] = (jnp.dot(ckvn, wk_ref[...], preferred_element_type=F32)
                + jnp.dot(krr, e_ref[...], preferred_element_type=F32)).astype(BF16)
    vlane = lax.broadcasted_iota(jnp.int32, (1, MLA_W), 1) & (HEAD_SLOT - 1)
    ones_lane = jnp.where(vlane == V_B, 1.0, 0.0)
    v_o[...] = (jnp.dot(ckvn, wv_ref[...], preferred_element_type=F32) + ones_lane).astype(BF16)


def _even_in(x, mod5, layer, b_off, g_pre, wp, tabs):
    B, S, D = x.shape
    tm = min(PROJ_TM, S)
    pos_spec = pl.BlockSpec((tm, LANES), lambda b, i: (i, 0))
    outs = [(256, BF16), (256, BF16), (512, BF16), (512, F32), (512, F32),
            (MLA_W, BF16), (MLA_W, BF16), (MLA_W, BF16)]
    return pl.pallas_call(
        _even_in_kernel,
        grid=(B, S // tm),
        in_specs=[
            _tok_spec(tm, D),
            _mod_spec(layer, 1, b_off),
            _resident((1, D)),
            _resident((D, EV_WIDTH)),
            _resident((LANES, 512)),
            _resident((1, 512)),
            _resident((1, Q_RANK)),
            _resident((Q_RANK, 2 * MLA_W)),
            _resident((1, KV_RANK)),
            _resident((KV_RANK, MLA_W)),
            _resident((KV_RANK, MLA_W)),
            _resident((LANES, MLA_W)),
            pos_spec, pos_spec, pos_spec, pos_spec,
        ],
        out_specs=[_tok_spec(tm, w) for w, _ in outs],
        out_shape=[jax.ShapeDtypeStruct((B, S, w), dt) for w, dt in outs],
        compiler_params=_cparams(("arbitrary", "arbitrary")),
        name="even_in",
    )(x, mod5, g_pre, wp["w_in"], wp["w_alpha"], wp["b_alpha"], wp["q_norm"], wp["w_q"], wp["kv_norm"],
      wp["w_k"], wp["w_v"], wp["e_rope"], tabs["cosq"], tabs["sinq"], tabs["cosk"], tabs["sink"])


def _dot_hilo(m, x):
    hi = x.astype(BF16)
    lo = (x - hi.astype(F32)).astype(BF16)
    return (jnp.dot(m, hi, preferred_element_type=F32) + jnp.dot(m, lo, preferred_element_type=F32))


def _gla_block(q, k, lg, v, cum, amask, state, reverse):
    ts = GLA_TS
    nchunk = ts // GLA_CHUNK
    edge = 0 if reverse else GLA_CHUNK - 1
    b = _dot_hilo(cum, lg)
    tot = jnp.concatenate(
        [jnp.broadcast_to(b[n * GLA_CHUNK + edge:n * GLA_CHUNK + edge + 1, :], (GLA_CHUNK, 2 * DK_A))
         for n in range(nchunk)], axis=0)
    bq = (b - lg) if reverse else b
    qt = q * jnp.exp(bq)
    kt = (k * jnp.exp(-b)).astype(BF16)
    kd = k * jnp.exp(tot - b)
    lane = lax.broadcasted_iota(jnp.int32, (ts, 2 * DK_A), 1)
    outs = []
    for hh in range(2):
        qh = jnp.where((lane >= hh * DK_A) & (lane < (hh + 1) * DK_A), qt, 0.0).astype(BF16)
        a = lax.dot_general(qh, kt, (((1,), (1,)), ((), ())), preferred_element_type=F32)
        a = jnp.where(amask, a, 0.0).astype(BF16)
        outs.append(jnp.dot(a, v[:, hh * DV_A:(hh + 1) * DV_A], preferred_element_type=F32))
    intra = jnp.concatenate(outs, axis=-1)
    kdt = kd.T.astype(BF16)
    bt = b.T
    row = lax.broadcasted_iota(jnp.int32, (ts, 2 * DV_A), 0)
    srow = lax.broadcasted_iota(jnp.int32, (2 * DK_A, 2 * DV_A), 0)
    scol = lax.broadcasted_iota(jnp.int32, (2 * DK_A, 2 * DV_A), 1)
    own = (srow >= DK_A) == (scol >= DV_A)
    qtb = qt.astype(BF16)
    inter = [None] * nchunk
    order = range(nchunk - 1, -1, -1) if reverse else range(nchunk)
    for n in order:
        r0 = n * GLA_CHUNK
        inter[n] = jnp.dot(qtb[r0:r0 + GLA_CHUNK, :], state.astype(BF16), preferred_element_type=F32)
        vn = jnp.where((row >= r0) & (row < r0 + GLA_CHUNK), v, jnp.zeros_like(v))
        kv = jnp.dot(kdt, vn, preferred_element_type=F32)
        dec = jnp.exp(bt[:, r0 + edge:r0 + edge + 1])
        state = jnp.where(own, dec * state + kv, 0.0)
    return intra + jnp.concatenate(inter, axis=0), state


def _gla_kernel(qf_ref, kf_ref, lgf_ref, vf_ref, qb_ref, kb_ref, lgb_ref, vb_ref,
                cumf_ref, cumb_ref, of_ref, ob_ref, sf_ref, sb_ref):
    @pl.when(pl.program_id(2) == 0)
    def _():
        sf_ref[...] = jnp.zeros_like(sf_ref)
        sb_ref[...] = jnp.zeros_like(sb_ref)

    ts = GLA_TS
    nblk = qf_ref.shape[0] // ts
    r = lax.broadcasted_iota(jnp.int32, (ts, ts), 0)
    c = lax.broadcasted_iota(jnp.int32, (ts, ts), 1)
    same = (r >> 6) == (c >> 6)
    mask_f = same & (c <= r)
    mask_b = same & (c > r)
    scale = DK_A ** -0.5
    state_f = sf_ref[...]
    state_b = sb_ref[...]
    for i in range(nblk):
        rf = slice(i * ts, (i + 1) * ts)
        out, state_f = _gla_block(qf_ref[rf, :].astype(F32) * scale, kf_ref[rf, :].astype(F32), lgf_ref[rf, :],
                                  vf_ref[rf, :], cumf_ref[...], mask_f, state_f, False)
        of_ref[rf, :] = out
        rb = slice((nblk - 1 - i) * ts, (nblk - i) * ts)
        out, state_b = _gla_block(qb_ref[rb, :].astype(F32) * scale, kb_ref[rb, :].astype(F32), lgb_ref[rb, :],
                                  vb_ref[rb, :], cumb_ref[...], mask_b, state_b, True)
        ob_ref[rb, :] = out
    sf_ref[...] = state_f
    sb_ref[...] = state_b


def _gla(qg, kg, vg, lg, consts):
    B, S, _ = qg.shape
    ts = min(GLA_STEP, S)
    nt = S // ts
    fwd = lambda w, off: pl.BlockSpec((None, ts, w), lambda b, p, t: (b, t, p + off))
    bwd = lambda w, off: pl.BlockSpec((None, ts, w), lambda b, p, t: (b, nt - 1 - t, p + off))
    npair = H_A // 2
    return pl.pallas_call(
        _gla_kernel,
        grid=(B, npair, nt),
        in_specs=[
            fwd(128, 0), fwd(128, 0), fwd(128, 0), fwd(256, 0),
            bwd(128, 0), bwd(128, 0), bwd(128, npair), bwd(256, 0),
            _resident((GLA_TS, GLA_TS)), _resident((GLA_TS, GLA_TS)),
        ],
        out_specs=[fwd(256, 0), bwd(256, 0)],
        out_shape=[jax.ShapeDtypeStruct((B, S, H_A * DV_A), F32)] * 2,
        scratch_shapes=[pltpu.VMEM((2 * DK_A, 2 * DV_A), F32)] * 2,
        compiler_params=_cparams(("arbitrary", "arbitrary", "arbitrary")),
        name="gla",
    )(qg, kg, lg, vg, qg, kg, lg, vg, consts["cum_f"], consts["cum_b"])


def _attn_kernel(q_ref, k_ref, v_ref, o_ref, *, tk, nk):
    tq = q_ref.shape[0]
    lanes = [slice(h * HEAD_SLOT, (h + 1) * HEAD_SLOT) for h in range(ATT_HEADS)]
    qs = [q_ref[:, sl] for sl in lanes]

    def body(j, carry):
        start = pl.multiple_of(j * tk, tk)
        new = []
        for h, sl in enumerate(lanes):
            m, acc = carry[h]
            s = lax.dot_general(qs[h], k_ref[pl.ds(start, tk), sl], (((1,), (1,)), ((), ())),
                                preferred_element_type=F32)
            m_new = jnp.maximum(m, jnp.max(s, axis=-1, keepdims=True))
            p = jnp.exp2(s - m_new).astype(BF16)
            acc = jnp.exp2(m - m_new) * acc + jnp.dot(p, v_ref[pl.ds(start, tk), sl], preferred_element_type=F32)
            new.append((m_new, acc))
        return tuple(new)

    init = tuple((jnp.full((tq, 1), -jnp.inf, F32), jnp.zeros((tq, HEAD_SLOT), F32)) for _ in lanes)
    res = lax.fori_loop(0, nk, body, init, unroll=2)
    for (_, acc), sl in zip(res, lanes):
        o_ref[:, sl] = (acc / acc[:, V_B:V_B + 1]).astype(BF16)


def _attn(q, k, v):
    B, S, _ = q.shape
    tq = min(ATT_TQ, S)
    tk = min(ATT_TK, S)
    w = ATT_HEADS * HEAD_SLOT
    kv_spec = pl.BlockSpec((None, S, w), lambda b, h, i: (b, 0, h), pipeline_mode=pl.Buffered(1))
    return pl.pallas_call(
        functools.partial(_attn_kernel, tk=tk, nk=S // tk),
        grid=(B, H_B // ATT_HEADS, S // tq),
        in_specs=[pl.BlockSpec((None, tq, w), lambda b, h, i: (b, i, h)), kv_spec, kv_spec],
        out_specs=pl.BlockSpec((None, tq, w), lambda b, h, i: (b, i, h)),
        out_shape=jax.ShapeDtypeStruct((B, S, MLA_W), BF16),
        compiler_params=_cparams(("arbitrary", "arbitrary", "arbitrary")),
        name="mla_attn",
    )(q, k, v)


def _even_out_kernel(x_ref, mod_ref, gpost_ref, of_ref, ob_ref, g_ref, gain_ref, ao_ref, wa_ref, wb_ref, o_ref):
    o = of_ref[...] + ob_ref[...]
    g = g_ref[...]
    gain = gain_ref[...]
    parts = []
    for hh in range(H_A):
        sl = slice(hh * DV_A, (hh + 1) * DV_A)
        parts.append(_rms(o[:, sl], gain) * _silu(g[:, sl]))
    oa = jnp.concatenate(parts, axis=-1).astype(BF16)
    mix = (jnp.dot(oa, wa_ref[...], preferred_element_type=F32)
           + jnp.dot(ao_ref[...], wb_ref[...], preferred_element_type=F32))
    o_ref[...] = _residual(x_ref[...], mix, mod_ref, gpost_ref[...], 1.0)


def _even_out(x, mod5, layer, b_off, g_post, o_f, o_b, g, gain, ao, wa, wb):
    B, S, D = x.shape
    tm = min(PROJ_TM, S)
    return pl.pallas_call(
        _even_out_kernel,
        grid=(B, S // tm),
        in_specs=[
            _tok_spec(tm, D), _mod_spec(layer, 1, b_off), _resident((1, D)),
            _tok_spec(tm, 512), _tok_spec(tm, 512), _tok_spec(tm, 512), _resident((1, DV_A)),
            _tok_spec(tm, MLA_W), _resident((H_A * DV_A, D)), _resident((MLA_W, D)),
        ],
        out_specs=_tok_spec(tm, D),
        out_shape=jax.ShapeDtypeStruct((B, S, D), F32),
        compiler_params=_cparams(("arbitrary", "arbitrary")),
        name="even_out",
    )(x, mod5, g_post, o_f, o_b, g, gain, ao, wa, wb)


def _odd_in_kernel(x_ref, mod_ref, gpre_ref, win_ref, sgun_ref, ws_ref, bs_ref, zc_o, od_o):
    x = x_ref[...]
    tm = x.shape[0]
    h = _modulated(x, mod_ref, gpre_ref[...]).astype(BF16)
    z = jnp.dot(h, win_ref[...], preferred_element_type=F32)
    wc = H_C * CG_C
    wd = H_D * DG_D
    zc_o[...] = z[:, :wc].astype(BF16)
    zd = z[:, wc:]
    ge = 0.5 * zd * (1.0 + lax.erf(zd * (2.0 ** -0.5)))
    u = ge[:, :wd]
    vn = _rms(ge[:, wd:], sgun_ref[...]).astype(BF16)
    for c in range(tm // SGU_CHUNK):
        rs = slice(c * SGU_CHUNK, (c + 1) * SGU_CHUNK)
        for hh in range(H_D):
            cs = slice(hh * DG_D, (hh + 1) * DG_D)
            sv = jnp.dot(ws_ref[hh], vn[rs, cs], preferred_element_type=F32) + bs_ref[hh]
            od_o[rs, cs] = (u[rs, cs] * sv).astype(BF16)


def _odd_in(x, mod5, layer, b_off, g_pre, w_in, sgu_norm, w_s, b_s):
    B, S, D = x.shape
    tm = min(PROJ_TM, S)
    return pl.pallas_call(
        _odd_in_kernel,
        grid=(B, S // tm),
        in_specs=[
            _tok_spec(tm, D), _mod_spec(layer, 1, b_off), _resident((1, D)),
            _resident((D, 1536)), _resident((1, 512)),
            _resident((H_D, SGU_CHUNK, SGU_CHUNK)), _resident((H_D, SGU_CHUNK, DG_D)),
        ],
        out_specs=[_tok_spec(tm, 512), _tok_spec(tm, 512)],
        out_shape=[jax.ShapeDtypeStruct((B, S, 512), BF16)] * 2,
        compiler_params=_cparams(("arbitrary", "arbitrary")),
        name="odd_in",
    )(x, mod5, g_pre, w_in, sgu_norm, w_s, b_s)


def _fnet1_kernel(f_ref, z_ref, a_ref):
    a_ref[...] = jnp.dot(f_ref[...], z_ref[...], preferred_element_type=F32).astype(BF16)


def _fnet2_kernel(a_ref, cs_ref, g_ref, o_ref, *, kt, n2, norm):
    wc = H_C * CG_C
    for j in range(kt):
        a2 = a_ref[:, j].reshape(2 * n2, wc)
        br, bi = [], []
        for gg in range(H_C):
            p = jnp.dot(a2[:, gg * CG_C:(gg + 1) * CG_C], cs_ref[...], preferred_element_type=F32)
            top, bot = p[:n2], p[n2:]
            br.append(top[:, :CG_C] + bot[:, CG_C:])
            bi.append(bot[:, :CG_C] - top[:, CG_C:])
        b2 = jnp.concatenate([jnp.concatenate(br, axis=-1), jnp.concatenate(bi, axis=-1)], axis=0).astype(BF16)
        zz = jnp.dot(g_ref[j], b2, preferred_element_type=F32)
        o_ref[:, j * wc:(j + 1) * wc] = (zz * norm).astype(BF16)


def _fnet(zc, consts):
    B, S, wc = zc.shape
    n1, n2 = consts["n1"], consts["n2"]
    tc = min(n2 * wc, 8192)
    a = pl.pallas_call(
        _fnet1_kernel,
        grid=(B, n2 * wc // tc),
        in_specs=[_resident((2 * n1, n1)), pl.BlockSpec((None, n1, tc), lambda b, j: (b, 0, j))],
        out_specs=pl.BlockSpec((None, 2 * n1, tc), lambda b, j: (b, 0, j)),
        out_shape=jax.ShapeDtypeStruct((B, 2 * n1, n2 * wc), BF16),
        compiler_params=_cparams(("arbitrary", "arbitrary")),
        name="fnet_stage1",
    )(consts["f1"], zc.reshape(B, n1, n2 * wc))
    kt = 8
    out = pl.pallas_call(
        functools.partial(_fnet2_kernel, kt=kt, n2=n2, norm=float(1.0 / math.sqrt(S * CG_C))),
        grid=(B, n1 // kt),
        in_specs=[
            pl.BlockSpec((None, 2, kt, n2, wc), lambda b, j: (b, 0, j, 0, 0)),
            _resident((CG_C, 2 * CG_C)),
            pl.BlockSpec((kt, n2, 2 * n2), lambda b, j: (j, 0, 0)),
        ],
        out_specs=pl.BlockSpec((None, n2, kt * wc), lambda b, j: (b, 0, j)),
        out_shape=jax.ShapeDtypeStruct((B, n2, n1 * wc), BF16),
        compiler_params=_cparams(("arbitrary", "arbitrary")),
        name="fnet_stage2",
    )(a.reshape(B, 2, n1, n2, wc), consts["cs"], consts["g"])
    return out.reshape(B, S, wc)


def _odd_out_kernel(x_ref, mod_ref, gpost_ref, fc_ref, od_ref, wa_ref, wb_ref, o_ref):
    mix = (jnp.dot(fc_ref[...], wa_ref[...], preferred_element_type=F32)
           + jnp.dot(od_ref[...], wb_ref[...], preferred_element_type=F32))
    o_ref[...] = _residual(x_ref[...], mix, mod_ref, gpost_ref[...], 1.0)


def _odd_out(x, mod5, layer, b_off, g_post, fc, od, wa, wb):
    B, S, D = x.shape
    tm = min(PROJ_TM, S)
    return pl.pallas_call(
        _odd_out_kernel,
        grid=(B, S // tm),
        in_specs=[
            _tok_spec(tm, D), _mod_spec(layer, 1, b_off), _resident((1, D)),
            _tok_spec(tm, 512), _tok_spec(tm, 512), _resident((512, D)), _resident((512, D)),
        ],
        out_specs=_tok_spec(tm, D),
        out_shape=jax.ShapeDtypeStruct((B, S, D), F32),
        compiler_params=_cparams(("arbitrary", "arbitrary")),
        name="odd_out",
    )(x, mod5, g_post, fc, od, wa, wb)


def _gla_consts():
    ts = GLA_TS
    r = np.arange(ts)[:, None]
    c = np.arange(ts)[None, :]
    same = (r // GLA_CHUNK) == (c // GLA_CHUNK)
    as_bf16 = lambda m: jnp.asarray(m.astype(np.float32), dtype=BF16)
    return {"cum_f": as_bf16(same & (c <= r)), "cum_b": as_bf16(same & (c >= r))}


def _rope_tables(S):
    half = ROPE_B // 2
    inv = ROPE_BASE ** (-jnp.arange(half, dtype=F32) / half)
    ang = jnp.arange(S, dtype=F32)[:, None] * inv[None, :]
    cos, sin = jnp.cos(ang), jnp.sin(ang)
    cc = jnp.concatenate([cos, cos], axis=-1)
    ss = jnp.concatenate([sin, sin], axis=-1)
    z = lambda w: jnp.zeros((S, w), F32)
    return {
        "cosq": jnp.concatenate([jnp.ones((S, NOPE_B), F32), cc, z(HEAD_SLOT - NOPE_B - ROPE_B)], axis=-1),
        "sinq": jnp.concatenate([z(NOPE_B), ss, z(HEAD_SLOT - NOPE_B - ROPE_B)], axis=-1),
        "cosk": jnp.concatenate([cc, z(LANES - ROPE_B)], axis=-1),
        "sink": jnp.concatenate([ss, z(LANES - ROPE_B)], axis=-1),
    }


def _fnet_consts(S):
    n1 = 1 << (int(math.log2(S)) // 2)
    n2 = S // n1
    two_pi = 2.0 * math.pi
    k1 = jnp.arange(n1, dtype=jnp.int32)
    ph1 = ((k1[:, None] * k1[None, :]) % n1).astype(F32) * (two_pi / n1)
    f1 = jnp.concatenate([jnp.cos(ph1), -jnp.sin(ph1)], axis=0).astype(BF16)
    ch = jnp.arange(CG_C, dtype=jnp.int32)
    phc = ((ch[:, None] * ch[None, :]) % CG_C).astype(F32) * (two_pi / CG_C)
    cs = jnp.concatenate([jnp.cos(phc), jnp.sin(phc)], axis=-1).astype(BF16)
    s2 = jnp.arange(n2, dtype=jnp.int32)
    k = k1[:, None, None] + n1 * s2[None, :, None]
    ph = ((k * s2[None, None, :]) % S).astype(F32) * (two_pi / S)
    g = jnp.concatenate([jnp.cos(ph), jnp.sin(ph)], axis=-1).astype(BF16)
    return {"n1": n1, "n2": n2, "f1": f1, "cs": cs, "g": g}


def _even_weights(w_in, w_out, w_alpha, b_alpha, gla_gain, q_norm, w_q_b, kv_norm, w_kv_b):
    D = D_MODEL
    offs = np.cumsum([0, 256, 256, 512, 512, 2 * ALPHA_RANK, Q_RANK, KV_RANK, ROPE_B])
    col = lambda i: w_in[:, offs[i]:offs[i + 1]]
    kr = col(7)
    half = ROPE_B // 2
    kr_rot = jnp.concatenate([-kr[:, half:], kr[:, :half]], axis=-1)
    zc = lambda w: jnp.zeros((D, w), F32)
    win = jnp.concatenate([col(0), col(1), col(2), col(3), col(5), col(6),
                           kr, col(4), zc(64), kr_rot, zc(96)], axis=-1).astype(BF16)
    assert win.shape[1] == EV_WIDTH
    wal = jnp.zeros((LANES, 512), F32)
    wal = wal.at[ROPE_B:ROPE_B + ALPHA_RANK, :256].set(w_alpha[0])
    wal = wal.at[ROPE_B + ALPHA_RANK:ROPE_B + 2 * ALPHA_RANK, 256:].set(w_alpha[1])
    bal = jnp.concatenate([b_alpha[0], b_alpha[1]])[None, :]
    wq = w_q_b.reshape(Q_RANK, H_B, NOPE_B + ROPE_B)
    qn, qr = wq[..., :NOPE_B], wq[..., NOPE_B:]
    pad = HEAD_SLOT - NOPE_B - ROPE_B
    zq = lambda w: jnp.zeros((Q_RANK, H_B, w), F32)
    wq_main = jnp.concatenate([qn, qr, zq(pad)], axis=-1).reshape(Q_RANK, MLA_W)
    wq_rot = jnp.concatenate([zq(NOPE_B), -qr[..., half:], qr[..., :half], zq(pad)], axis=-1).reshape(Q_RANK, MLA_W)
    wq2 = jnp.concatenate([wq_main, wq_rot], axis=-1).astype(BF16)
    wkv = w_kv_b.reshape(KV_RANK, H_B, NOPE_B + V_B)
    zk = lambda w: jnp.zeros((KV_RANK, H_B, w), F32)
    wk = jnp.concatenate([wkv[..., :NOPE_B], zk(HEAD_SLOT - NOPE_B)], axis=-1).reshape(KV_RANK, MLA_W).astype(BF16)
    wv = jnp.concatenate([wkv[..., NOPE_B:], zk(HEAD_SLOT - V_B)], axis=-1).reshape(KV_RANK, MLA_W).astype(BF16)
    e = np.zeros((LANES, H_B, HEAD_SLOT), np.float32)
    for r in range(ROPE_B):
        e[r, :, NOPE_B + r] = 1.0
    e = jnp.asarray(e.reshape(LANES, MLA_W), dtype=BF16)
    wo_a = w_out[:H_A * DV_A].astype(BF16)
    wo_b = jnp.concatenate([w_out[H_A * DV_A:].reshape(H_B, V_B, D),
                            jnp.zeros((H_B, HEAD_SLOT - V_B, D), F32)], axis=1).reshape(MLA_W, D).astype(BF16)
    return {"w_in": win, "w_alpha": wal.astype(BF16), "b_alpha": bal, "q_norm": q_norm[None, :], "w_q": wq2,
            "kv_norm": kv_norm[None, :], "w_k": wk, "w_v": wv, "e_rope": e, "gain": gla_gain[None, :],
            "wo_a": wo_a, "wo_b": wo_b}


def kernel(x_prompt, x_sample, c_prompt, c_sample, ada_w, ada_b, norm_pre, norm_post, ffn_w13, ffn_w2,
           ev_w_in, ev_w_out, gla_w_alpha, gla_b_alpha, gla_norm, mla_q_norm, mla_w_q_b, mla_kv_norm,
           mla_w_kv_b, od_w_in, od_w_out, sgu_norm, sgu_w_s, sgu_b):
    groups = [(x_prompt, 0), (x_sample, x_prompt.shape[0])]
    c_all = jnp.concatenate([c_prompt, c_sample], axis=0)
    mod5 = _ada(c_all, ada_w, ada_b).reshape(DEPTH, c_all.shape[0], 3, 3, D_MODEL)

    w13 = ffn_w13.astype(BF16)
    w2 = ffn_w2.astype(BF16)
    gla_consts = _gla_consts()
    rope = {x.shape[1]: _rope_tables(x.shape[1]) for x, _ in groups}
    fnet = {x.shape[1]: _fnet_consts(x.shape[1]) for x, _ in groups}
    even_w = [_even_weights(ev_w_in[i], ev_w_out[i], gla_w_alpha[i], gla_b_alpha[i], gla_norm[i], mla_q_norm[i],
                            mla_w_q_b[i], mla_kv_norm[i], mla_w_kv_b[i]) for i in range(ev_w_in.shape[0])]
    od_win = od_w_in.astype(BF16)
    od_wout = od_w_out.astype(BF16)
    sgu_ws = sgu_w_s.astype(BF16)
    sgu_bias = jnp.broadcast_to(sgu_b[..., None], sgu_b.shape + (DG_D,))

    outs = []
    for x, b_off in groups:
        S = x.shape[1]
        for l in range(DEPTH):
            pre = lambda s: norm_pre[l, s][None, :]
            post = lambda s: norm_post[l, s][None, :]
            x = _ffn(x, mod5, l, 0, 0, b_off, pre(0), post(0), w13, w2)
            i = l // 2
            if l % 2 == 0:
                wp = even_w[i]
                qg, kg, vg, g, lg, q, k, v = _even_in(x, mod5, l, b_off, pre(1), wp, rope[S])
                o_f, o_b = _gla(qg, kg, vg, lg, gla_consts)
                ao = _attn(q, k, v)
                x = _even_out(x, mod5, l, b_off, post(1), o_f, o_b, g, wp["gain"], ao, wp["wo_a"], wp["wo_b"])
            else:
                zc, od = _odd_in(x, mod5, l, b_off, pre(1), od_win[i], sgu_norm[i][None, :], sgu_ws[i], sgu_bias[i])
                fc = _fnet(zc, fnet[S])
                x = _odd_out(x, mod5, l, b_off, post(1), fc, od, od_wout[i, :512], od_wout[i, 512:])
            x = _ffn(x, mod5, l, 2, 1, b_off, pre(2), post(2), w13, w2)
        outs.append(x)
    return tuple(outs)
```

```python
import functools
import math

import jax
import jax.numpy as jnp
import numpy as np
from jax import lax
from jax.experimental import pallas as pl
from jax.experimental.pallas import tpu as pltpu

F32 = jnp.float32
BF16 = jnp.bfloat16

D_MODEL = 1024
DEPTH = 4
D_FF = 2816
EPS = 1e-6
H_A, DK_A, DV_A = 4, 64, 128
ALPHA_RANK = 16
GATE_NORM = 16.0
GLA_CHUNK = 64
H_B, Q_RANK, KV_RANK, NOPE_B, ROPE_B, V_B = 8, 256, 128, 64, 32, 64
ROPE_BASE = 10000.0
H_C, CG_C = 4, 128
H_D, DG_D = 4, 128
SGU_CHUNK = 128
HEAD_SLOT = 128
MLA_W = H_B * HEAD_SLOT
W_C = H_C * CG_C
W_D = H_D * DG_D

VMEM_LIMIT_BYTES = 56 * 1024 * 1024
LANES = 128

CHAIN_TM = 512
CHAIN_TM_EVEN_IN = 256
CHAIN_ROWS = 256
FFN_CHUNKS = (1024, 1024, 768)
GLA_TS = 256
GLA_STEP = 1024
ATT_TQ = 512
ATT_TK = 2048
ATT_HEADS = 2
ATT_UNROLL = 2
ATT_AHEAD = 1


def _cparams(sem):
    return pltpu.CompilerParams(dimension_semantics=sem, vmem_limit_bytes=VMEM_LIMIT_BYTES)


def _resident(shape):
    nd = len(shape)
    return pl.BlockSpec(shape, lambda *_: (0,) * nd, pipeline_mode=pl.Buffered(1))


def _rms(x, g):
    return x * lax.rsqrt(jnp.mean(x * x, axis=-1, keepdims=True) + EPS) * g


def _modulated(x, mod_ref, g_pre):
    shift = mod_ref[0:1, :]
    scale = mod_ref[1:2, :]
    return _rms(x, g_pre) * (1.0 + scale) + shift


def _residual(x, y, mod_ref, g_post, w_res):
    gate = mod_ref[2:3, :]
    return x + (w_res * (1.0 + gate)) * _rms(y, g_post)


def _silu(a):
    return a * jax.nn.sigmoid(a)


def _mod_spec(layer, sub, b_off):
    return pl.BlockSpec((None, None, None, 3, D_MODEL), lambda b, i: (layer, b_off + b, sub, 0, 0))


def _tok_spec(tm, width):
    return pl.BlockSpec((None, tm, width), lambda b, i: (b, i, 0))


def _ada_kernel(c_ref, w_ref, b_ref, o_ref):
    cc = _silu(c_ref[...]).astype(BF16)
    o_ref[...] = jnp.dot(cc, w_ref[...].astype(BF16), preferred_element_type=F32) + b_ref[...]


def _ada(c_all, ada_w, ada_b):
    bt = c_all.shape[0]
    n = ada_w.shape[-1]
    tn = n // 8
    return pl.pallas_call(
        _ada_kernel,
        grid=(DEPTH, n // tn),
        in_specs=[
            pl.BlockSpec((bt, D_MODEL), lambda l, j: (0, 0)),
            pl.BlockSpec((None, D_MODEL, tn), lambda l, j: (l, 0, j)),
            pl.BlockSpec((None, 1, tn), lambda l, j: (l, 0, j)),
        ],
        out_specs=pl.BlockSpec((None, bt, tn), lambda l, j: (l, 0, j)),
        out_shape=jax.ShapeDtypeStruct((DEPTH, bt, n), F32),
        compiler_params=_cparams(("arbitrary", "arbitrary")),
        name="ada_mod",
    )(c_all, ada_w, ada_b.reshape(DEPTH, 1, n))


def _ffn_stage(x, refs, outs, rs):
    mod_ref, gpre_ref, gpost_ref, w13_ref, w2_ref = refs
    h = _modulated(x, mod_ref, gpre_ref[...]).astype(BF16)
    acc = None
    off = 0
    for ck in FFN_CHUNKS:
        a = jnp.dot(h, w13_ref[:, off:off + ck], preferred_element_type=F32)
        b = jnp.dot(h, w13_ref[:, D_FF + off:D_FF + off + ck], preferred_element_type=F32)
        act = (_silu(a) * b).astype(BF16)
        part = jnp.dot(act, w2_ref[off:off + ck, :], preferred_element_type=F32)
        acc = part if acc is None else acc + part
        off += ck
    return _residual(x, acc, mod_ref, gpost_ref[...], 0.5)


EV_Q, EV_K, EV_V, EV_G, EV_CQ, EV_CKV, EV_A, EV_B = 0, 256, 512, 1024, 1536, 1792, 1920, 2048
EV_WIDTH = 2176


def _even_in_stage(x, refs, outs, rs):
    (mod_ref, gpre_ref, win_ref, walpha_ref, balpha_ref, qn_ref, wq_ref, kvn_ref, wk_ref, wv_ref, e_ref,
     cosq_ref, sinq_ref, cosk_ref, sink_ref) = refs
    qg_o, kg_o, vg_o, g_o, lg_o, q_o, k_o, vt_o = outs
    h = _modulated(x, mod_ref, gpre_ref[...]).astype(BF16)
    z = jnp.dot(h, win_ref[...], preferred_element_type=F32)
    qg_o[rs, :] = z[:, EV_Q:EV_K].astype(BF16)
    kg_o[rs, :] = z[:, EV_K:EV_V].astype(BF16)
    vg_o[rs, :] = z[:, EV_V:EV_G].astype(BF16)
    g_o[rs, :] = z[:, EV_G:EV_CQ]
    cq = z[:, EV_CQ:EV_CKV]
    ckv = z[:, EV_CKV:EV_A]
    blk_a = z[:, EV_A:EV_B]
    blk_b = z[:, EV_B:EV_WIDTH]
    al = jnp.dot(blk_a.astype(BF16), walpha_ref[...], preferred_element_type=F32) + balpha_ref[...]
    lg_o[rs, :] = (jnp.minimum(al, 0.0) - jnp.log1p(jnp.exp(-jnp.abs(al)))) * (1.0 / GATE_NORM)
    cqn = _rms(cq, qn_ref[...]).astype(BF16)
    q2 = jnp.dot(cqn, wq_ref[...], preferred_element_type=F32)
    cosq = cosq_ref[rs, :]
    sinq = sinq_ref[rs, :]
    scale = (NOPE_B + ROPE_B) ** -0.5 * math.log2(math.e)
    for hh in range(H_B):
        lo = hh * HEAD_SLOT
        qs = q2[:, lo:lo + HEAD_SLOT] * cosq + q2[:, MLA_W + lo:MLA_W + lo + HEAD_SLOT] * sinq
        q_o[rs, lo:lo + HEAD_SLOT] = (qs * scale).astype(BF16)
    ckvn = _rms(ckv, kvn_ref[...]).astype(BF16)
    krr = (blk_a * cosk_ref[rs, :] + blk_b * sink_ref[rs, :]).astype(BF16)
    k_o[rs, :] = (jnp.dot(ckvn, wk_ref[...], preferred_element_type=F32)
                  + jnp.dot(krr, e_ref[...], preferred_element_type=F32)).astype(BF16)
    vlane = lax.broadcasted_iota(jnp.int32, (1, MLA_W), 1) & (HEAD_SLOT - 1)
    ones_lane = jnp.where(vlane == V_B, 1.0, 0.0)
    v = jnp.dot(ckvn, wv_ref[...], preferred_element_type=F32) + ones_lane
    vt_o[:, rs] = v.T.astype(BF16)
    return x


def _even_out_stage(x, refs, outs, rs):
    mod_ref, gpost_ref, of_ref, ob_ref, g_ref, gain_ref, ao_ref, wa_ref, wb_ref = refs
    o = of_ref[rs, :] + ob_ref[rs, :]
    g = g_ref[rs, :]
    gain = gain_ref[...]
    parts = []
    for hh in range(H_A):
        sl = slice(hh * DV_A, (hh + 1) * DV_A)
        parts.append(_rms(o[:, sl], gain) * _silu(g[:, sl]))
    oa = jnp.concatenate(parts, axis=-1).astype(BF16)
    mix = (jnp.dot(oa, wa_ref[...], preferred_element_type=F32)
           + jnp.dot(ao_ref[rs, :], wb_ref[...], preferred_element_type=F32))
    return _residual(x, mix, mod_ref, gpost_ref[...], 1.0)


def _odd_in_stage(x, refs, outs, rs):
    mod_ref, gpre_ref, win_ref, sgun_ref, ws_ref, bs_ref = refs
    zc_o, od_o = outs
    h = _modulated(x, mod_ref, gpre_ref[...]).astype(BF16)
    z = jnp.dot(h, win_ref[...], preferred_element_type=F32)
    zc_o[rs, :] = z[:, :W_C].astype(BF16)
    zd = z[:, W_C:]
    ge = 0.5 * zd * (1.0 + lax.erf(zd * (2.0 ** -0.5)))
    u = ge[:, :W_D]
    vn = _rms(ge[:, W_D:], sgun_ref[...]).astype(BF16)
    for c in range(x.shape[0] // SGU_CHUNK):
        cr = slice(c * SGU_CHUNK, (c + 1) * SGU_CHUNK)
        orow = slice(rs.start + c * SGU_CHUNK, rs.start + (c + 1) * SGU_CHUNK)
        for hh in range(H_D):
            cs = slice(hh * DG_D, (hh + 1) * DG_D)
            sv = jnp.dot(ws_ref[hh], vn[cr, cs], preferred_element_type=F32) + bs_ref[hh]
            od_o[orow, cs] = (u[cr, cs] * sv).astype(BF16)
    return x


def _odd_out_stage(x, refs, outs, rs):
    mod_ref, gpost_ref, fc_ref, od_ref, wa_ref, wb_ref = refs
    mix = (jnp.dot(fc_ref[rs, :], wa_ref[...], preferred_element_type=F32)
           + jnp.dot(od_ref[rs, :], wb_ref[...], preferred_element_type=F32))
    return _residual(x, mix, mod_ref, gpost_ref[...], 1.0)


def _chain_kernel(*refs, fns, counts, rows, write_x):
    n_in = 1 + sum(c[0] for c in counts)
    x_ref = refs[0]
    stage_refs = []
    i, o = 1, n_in + int(write_x)
    for n_args, n_outs in counts:
        stage_refs.append((refs[i:i + n_args], refs[o:o + n_outs]))
        i += n_args
        o += n_outs
    for r in range(x_ref.shape[0] // rows):
        rs = slice(r * rows, (r + 1) * rows)
        x = x_ref[rs, :]
        for fn, (ins, outs) in zip(fns, stage_refs):
            x = fn(x, ins, outs, rs)
        if write_x:
            refs[n_in][rs, :] = x


def _chain(x, stages, tm, name, write_x=True):
    B, S, D = x.shape
    tm = min(tm, S)
    arrays, specs = [x], [_tok_spec(tm, D)]
    out_shapes = [jax.ShapeDtypeStruct((B, S, D), F32)] if write_x else []
    out_specs = [_tok_spec(tm, D)] if write_x else []
    counts = []
    for _, ins, outs in stages:
        arrays += [a for a, _ in ins]
        specs += [s(tm) for _, s in ins]
        for w, dt, *transposed in outs:
            if transposed:
                out_shapes.append(jax.ShapeDtypeStruct((B, w, S), dt))
                out_specs.append(pl.BlockSpec((None, w, tm), lambda b, i: (b, 0, i)))
            else:
                out_shapes.append(jax.ShapeDtypeStruct((B, S, w), dt))
                out_specs.append(_tok_spec(tm, w))
        counts.append((len(ins), len(outs)))
    return pl.pallas_call(
        functools.partial(_chain_kernel, fns=tuple(s[0] for s in stages), counts=tuple(counts),
                          rows=min(CHAIN_ROWS, tm), write_x=write_x),
        grid=(B, S // tm),
        in_specs=specs,
        out_specs=out_specs,
        out_shape=out_shapes,
        compiler_params=_cparams(("arbitrary", "arbitrary")),
        name=name,
    )(*arrays)


def _res(a):
    return (a, lambda tm: _resident(a.shape))


def _tok(a):
    return (a, lambda tm: _tok_spec(tm, a.shape[-1]))


def _pos(a):
    return (a, lambda tm: pl.BlockSpec((tm, a.shape[-1]), lambda b, i: (i, 0)))


def _ffn_stage_spec(mod5, layer, sub, which, b_off, g_pre, g_post, w13, w2):
    wspec = lambda r, c: (lambda tm: pl.BlockSpec((None, None, r, c), lambda b, i: (layer, which, 0, 0),
                                                  pipeline_mode=pl.Buffered(1)))
    ins = [(mod5, lambda tm: _mod_spec(layer, sub, b_off)), _res(g_pre), _res(g_post),
           (w13, wspec(D_MODEL, 2 * D_FF)), (w2, wspec(D_FF, D_MODEL))]
    return (_ffn_stage, ins, [])


def _even_in_stage_spec(mod5, layer, b_off, g_pre, wp, tabs):
    ins = [(mod5, lambda tm: _mod_spec(layer, 1, b_off)), _res(g_pre), _res(wp["w_in"]), _res(wp["w_alpha"]),
           _res(wp["b_alpha"]), _res(wp["q_norm"]), _res(wp["w_q"]), _res(wp["kv_norm"]), _res(wp["w_k"]),
           _res(wp["w_v"]), _res(wp["e_rope"]),
           _pos(tabs["cosq"]), _pos(tabs["sinq"]), _pos(tabs["cosk"]), _pos(tabs["sink"])]
    outs = [(256, BF16), (256, BF16), (512, BF16), (512, F32), (512, F32), (MLA_W, BF16), (MLA_W, BF16),
            (MLA_W, BF16, True)]
    return (_even_in_stage, ins, outs)


def _even_out_stage_spec(mod5, layer, b_off, g_post, o_f, o_b, g, ao, wp):
    ins = [(mod5, lambda tm: _mod_spec(layer, 1, b_off)), _res(g_post), _tok(o_f), _tok(o_b), _tok(g),
           _res(wp["gain"]), _tok(ao), _res(wp["wo_a"]), _res(wp["wo_b"])]
    return (_even_out_stage, ins, [])


def _odd_in_stage_spec(mod5, layer, b_off, g_pre, w_in, sgu_norm, w_s, b_s):
    ins = [(mod5, lambda tm: _mod_spec(layer, 1, b_off)), _res(g_pre), _res(w_in), _res(sgu_norm), _res(w_s), _res(b_s)]
    return (_odd_in_stage, ins, [(W_C, BF16), (W_D, BF16)])


def _odd_out_stage_spec(mod5, layer, b_off, g_post, fc, od, wa, wb):
    ins = [(mod5, lambda tm: _mod_spec(layer, 1, b_off)), _res(g_post), _tok(fc), _tok(od), _res(wa), _res(wb)]
    return (_odd_out_stage, ins, [])


def _dot_hilo(m, x):
    hi = x.astype(BF16)
    lo = (x - hi.astype(F32)).astype(BF16)
    return (jnp.dot(m, hi, preferred_element_type=F32) + jnp.dot(m, lo, preferred_element_type=F32))


def _gla_block(q, k, lg, v, cum, amask, state, reverse):
    ts = GLA_TS
    nchunk = ts // GLA_CHUNK
    edge = 0 if reverse else GLA_CHUNK - 1
    b = _dot_hilo(cum, lg)
    tot = jnp.concatenate(
        [jnp.broadcast_to(b[n * GLA_CHUNK + edge:n * GLA_CHUNK + edge + 1, :], (GLA_CHUNK, 2 * DK_A))
         for n in range(nchunk)], axis=0)
    bq = (b - lg) if reverse else b
    qt = q * jnp.exp(bq)
    kt = (k * jnp.exp(-b)).astype(BF16)
    kd = k * jnp.exp(tot - b)
    lane = lax.broadcasted_iota(jnp.int32, (ts, 2 * DK_A), 1)
    outs = []
    for hh in range(2):
        qh = jnp.where((lane >= hh * DK_A) & (lane < (hh + 1) * DK_A), qt, 0.0).astype(BF16)
        a = lax.dot_general(qh, kt, (((1,), (1,)), ((), ())), preferred_element_type=F32)
        a = jnp.where(amask, a, 0.0).astype(BF16)
        outs.append(jnp.dot(a, v[:, hh * DV_A:(hh + 1) * DV_A], preferred_element_type=F32))
    intra = jnp.concatenate(outs, axis=-1)
    kdt = kd.T.astype(BF16)
    bt = b.T
    row = lax.broadcasted_iota(jnp.int32, (ts, 2 * DV_A), 0)
    srow = lax.broadcasted_iota(jnp.int32, (2 * DK_A, 2 * DV_A), 0)
    scol = lax.broadcasted_iota(jnp.int32, (2 * DK_A, 2 * DV_A), 1)
    own = (srow >= DK_A) == (scol >= DV_A)
    qtb = qt.astype(BF16)
    inter = [None] * nchunk
    order = range(nchunk - 1, -1, -1) if reverse else range(nchunk)
    for n in order:
        r0 = n * GLA_CHUNK
        inter[n] = jnp.dot(qtb[r0:r0 + GLA_CHUNK, :], state.astype(BF16), preferred_element_type=F32)
        vn = jnp.where((row >= r0) & (row < r0 + GLA_CHUNK), v, jnp.zeros_like(v))
        kv = jnp.dot(kdt, vn, preferred_element_type=F32)
        dec = jnp.exp(bt[:, r0 + edge:r0 + edge + 1])
        state = jnp.where(own, dec * state + kv, 0.0)
    return intra + jnp.concatenate(inter, axis=0), state


def _gla_kernel(qf_ref, kf_ref, lgf_ref, vf_ref, qb_ref, kb_ref, lgb_ref, vb_ref,
                cumf_ref, cumb_ref, of_ref, ob_ref, sf_ref, sb_ref):
    @pl.when(pl.program_id(2) == 0)
    def _():
        sf_ref[...] = jnp.zeros_like(sf_ref)
        sb_ref[...] = jnp.zeros_like(sb_ref)

    ts = GLA_TS
    nblk = qf_ref.shape[0] // ts
    r = lax.broadcasted_iota(jnp.int32, (ts, ts), 0)
    c = lax.broadcasted_iota(jnp.int32, (ts, ts), 1)
    same = (r >> 6) == (c >> 6)
    mask_f = same & (c <= r)
    mask_b = same & (c > r)
    scale = DK_A ** -0.5
    state_f = sf_ref[...]
    state_b = sb_ref[...]
    for i in range(nblk):
        rf = slice(i * ts, (i + 1) * ts)
        out, state_f = _gla_block(qf_ref[rf, :].astype(F32) * scale, kf_ref[rf, :].astype(F32), lgf_ref[rf, :],
                                  vf_ref[rf, :], cumf_ref[...], mask_f, state_f, False)
        of_ref[rf, :] = out
        rb = slice((nblk - 1 - i) * ts, (nblk - i) * ts)
        out, state_b = _gla_block(qb_ref[rb, :].astype(F32) * scale, kb_ref[rb, :].astype(F32), lgb_ref[rb, :],
                                  vb_ref[rb, :], cumb_ref[...], mask_b, state_b, True)
        ob_ref[rb, :] = out
    sf_ref[...] = state_f
    sb_ref[...] = state_b


def _gla(qg, kg, vg, lg, consts):
    B, S, _ = qg.shape
    ts = min(GLA_STEP, S)
    nt = S // ts
    fwd = lambda w, off: pl.BlockSpec((None, ts, w), lambda b, p, t: (b, t, p + off))
    bwd = lambda w, off: pl.BlockSpec((None, ts, w), lambda b, p, t: (b, nt - 1 - t, p + off))
    npair = H_A // 2
    return pl.pallas_call(
        _gla_kernel,
        grid=(B, npair, nt),
        in_specs=[
            fwd(128, 0), fwd(128, 0), fwd(128, 0), fwd(256, 0),
            bwd(128, 0), bwd(128, 0), bwd(128, npair), bwd(256, 0),
            _resident((GLA_TS, GLA_TS)), _resident((GLA_TS, GLA_TS)),
        ],
        out_specs=[fwd(256, 0), bwd(256, 0)],
        out_shape=[jax.ShapeDtypeStruct((B, S, H_A * DV_A), F32)] * 2,
        scratch_shapes=[pltpu.VMEM((2 * DK_A, 2 * DV_A), F32)] * 2,
        compiler_params=_cparams(("arbitrary", "arbitrary", "arbitrary")),
        name="gla",
    )(qg, kg, lg, vg, qg, kg, lg, vg, consts["cum_f"], consts["cum_b"])


def _attn_kernel(q_ref, k_ref, vt_ref, o_ref, *, tk, nk):
    tq = q_ref.shape[0]
    slots = [slice(h * HEAD_SLOT, (h + 1) * HEAD_SLOT) for h in range(ATT_HEADS)]
    qts = [q_ref[:, sl].astype(F32).T.astype(BF16) for sl in slots]

    group = min(ATT_UNROLL, nk)

    def scores(h, start):
        return jnp.dot(k_ref[pl.ds(start, tk), slots[h]], qts[h], preferred_element_type=F32)

    def absorb(h, start, st, state):
        m, acc = state
        m_new = jnp.maximum(m, jnp.max(st, axis=0, keepdims=True))
        pt = jnp.exp2(st - m_new).astype(BF16)
        acc = jnp.exp2(m - m_new) * acc + jnp.dot(vt_ref[slots[h], pl.ds(start, tk)], pt,
                                                  preferred_element_type=F32)
        return m_new, acc

    def body(j, carry):
        items = [(pl.multiple_of((j * group + g) * tk, tk), h) for g in range(group) for h in range(ATT_HEADS)]
        state = list(carry)
        pending = [scores(h, start) for start, h in items[:ATT_AHEAD]]
        for i, (start, h) in enumerate(items):
            if i + ATT_AHEAD < len(items):
                nstart, nh = items[i + ATT_AHEAD]
                pending.append(scores(nh, nstart))
            state[h] = absorb(h, start, pending.pop(0), state[h])
        return tuple(state)

    init = tuple((jnp.full((1, tq), -jnp.inf, F32), jnp.zeros((HEAD_SLOT, tq), F32)) for _ in slots)
    res = lax.fori_loop(0, nk // group, body, init)
    for (_, acc), sl in zip(res, slots):
        o_ref[:, sl] = (acc / acc[V_B:V_B + 1, :]).T.astype(BF16)


def _attn(q, k, vt):
    B, S, _ = q.shape
    tq = min(ATT_TQ, S)
    tk = min(ATT_TK, S)
    w = ATT_HEADS * HEAD_SLOT
    return pl.pallas_call(
        functools.partial(_attn_kernel, tk=tk, nk=S // tk),
        grid=(B, H_B // ATT_HEADS, S // tq),
        in_specs=[
            pl.BlockSpec((None, tq, w), lambda b, h, i: (b, i, h)),
            pl.BlockSpec((None, S, w), lambda b, h, i: (b, 0, h), pipeline_mode=pl.Buffered(1)),
            pl.BlockSpec((None, w, S), lambda b, h, i: (b, h, 0), pipeline_mode=pl.Buffered(1)),
        ],
        out_specs=pl.BlockSpec((None, tq, w), lambda b, h, i: (b, i, h)),
        out_shape=jax.ShapeDtypeStruct((B, S, MLA_W), BF16),
        compiler_params=_cparams(("arbitrary", "arbitrary", "arbitrary")),
        name="mla_attn",
    )(q, k, vt)


def _fnet1_kernel(f_ref, z_ref, a_ref):
    a_ref[...] = jnp.dot(f_ref[...], z_ref[...], preferred_element_type=F32).astype(BF16)


def _fnet2_kernel(a_ref, cs_ref, g_ref, o_ref, *, kt, n2, norm):
    for j in range(kt):
        a2 = a_ref[:, j].reshape(2 * n2, W_C)
        br, bi = [], []
        for gg in range(H_C):
            p = jnp.dot(a2[:, gg * CG_C:(gg + 1) * CG_C], cs_ref[...], preferred_element_type=F32)
            top, bot = p[:n2], p[n2:]
            br.append(top[:, :CG_C] + bot[:, CG_C:])
            bi.append(bot[:, :CG_C] - top[:, CG_C:])
        b2 = jnp.concatenate([jnp.concatenate(br, axis=-1), jnp.concatenate(bi, axis=-1)], axis=0).astype(BF16)
        zz = jnp.dot(g_ref[j], b2, preferred_element_type=F32)
        o_ref[:, j * W_C:(j + 1) * W_C] = (zz * norm).astype(BF16)


def _fnet(zc, consts):
    B, S, wc = zc.shape
    n1, n2 = consts["n1"], consts["n2"]
    tc = min(n2 * wc, 8192)
    a = pl.pallas_call(
        _fnet1_kernel,
        grid=(B, n2 * wc // tc),
        in_specs=[_resident((2 * n1, n1)), pl.BlockSpec((None, n1, tc), lambda b, j: (b, 0, j))],
        out_specs=pl.BlockSpec((None, 2 * n1, tc), lambda b, j: (b, 0, j)),
        out_shape=jax.ShapeDtypeStruct((B, 2 * n1, n2 * wc), BF16),
        compiler_params=_cparams(("arbitrary", "arbitrary")),
        name="fnet_stage1",
    )(consts["f1"], zc.reshape(B, n1, n2 * wc))
    kt = 8
    out = pl.pallas_call(
        functools.partial(_fnet2_kernel, kt=kt, n2=n2, norm=float(1.0 / math.sqrt(S * CG_C))),
        grid=(B, n1 // kt),
        in_specs=[
            pl.BlockSpec((None, 2, kt, n2, wc), lambda b, j: (b, 0, j, 0, 0)),
            _resident((CG_C, 2 * CG_C)),
            pl.BlockSpec((kt, n2, 2 * n2), lambda b, j: (j, 0, 0)),
        ],
        out_specs=pl.BlockSpec((None, n2, kt * wc), lambda b, j: (b, 0, j)),
        out_shape=jax.ShapeDtypeStruct((B, n2, n1 * wc), BF16),
        compiler_params=_cparams(("arbitrary", "arbitrary")),
        name="fnet_stage2",
    )(a.reshape(B, 2, n1, n2, wc), consts["cs"], consts["g"])
    return out.reshape(B, S, wc)


def _gla_consts():
    ts = GLA_TS
    r = np.arange(ts)[:, None]
    c = np.arange(ts)[None, :]
    same = (r // GLA_CHUNK) == (c // GLA_CHUNK)
    as_bf16 = lambda m: jnp.asarray(m.astype(np.float32), dtype=BF16)
    return {"cum_f": as_bf16(same & (c <= r)), "cum_b": as_bf16(same & (c >= r))}


def _rope_tables(S):
    half = ROPE_B // 2
    inv = ROPE_BASE ** (-jnp.arange(half, dtype=F32) / half)
    ang = jnp.arange(S, dtype=F32)[:, None] * inv[None, :]
    cos, sin = jnp.cos(ang), jnp.sin(ang)
    cc = jnp.concatenate([cos, cos], axis=-1)
    ss = jnp.concatenate([sin, sin], axis=-1)
    z = lambda w: jnp.zeros((S, w), F32)
    return {
        "cosq": jnp.concatenate([jnp.ones((S, NOPE_B), F32), cc, z(HEAD_SLOT - NOPE_B - ROPE_B)], axis=-1),
        "sinq": jnp.concatenate([z(NOPE_B), ss, z(HEAD_SLOT - NOPE_B - ROPE_B)], axis=-1),
        "cosk": jnp.concatenate([cc, z(LANES - ROPE_B)], axis=-1),
        "sink": jnp.concatenate([ss, z(LANES - ROPE_B)], axis=-1),
    }


def _fnet_consts(S):
    n1 = 1 << (int(math.log2(S)) // 2)
    n2 = S // n1
    two_pi = 2.0 * math.pi
    k1 = jnp.arange(n1, dtype=jnp.int32)
    ph1 = ((k1[:, None] * k1[None, :]) % n1).astype(F32) * (two_pi / n1)
    f1 = jnp.concatenate([jnp.cos(ph1), -jnp.sin(ph1)], axis=0).astype(BF16)
    ch = jnp.arange(CG_C, dtype=jnp.int32)
    phc = ((ch[:, None] * ch[None, :]) % CG_C).astype(F32) * (two_pi / CG_C)
    cs = jnp.concatenate([jnp.cos(phc), jnp.sin(phc)], axis=-1).astype(BF16)
    s2 = jnp.arange(n2, dtype=jnp.int32)
    k = k1[:, None, None] + n1 * s2[None, :, None]
    ph = ((k * s2[None, None, :]) % S).astype(F32) * (two_pi / S)
    g = jnp.concatenate([jnp.cos(ph), jnp.sin(ph)], axis=-1).astype(BF16)
    return {"n1": n1, "n2": n2, "f1": f1, "cs": cs, "g": g}


def _even_weights(w_in, w_out, w_alpha, b_alpha, gla_gain, q_norm, w_q_b, kv_norm, w_kv_b):
    D = D_MODEL
    offs = np.cumsum([0, 256, 256, 512, 512, 2 * ALPHA_RANK, Q_RANK, KV_RANK, ROPE_B])
    col = lambda i: w_in[:, offs[i]:offs[i + 1]]
    kr = col(7)
    half = ROPE_B // 2
    kr_rot = jnp.concatenate([-kr[:, half:], kr[:, :half]], axis=-1)
    zc = lambda w: jnp.zeros((D, w), F32)
    win = jnp.concatenate([col(0), col(1), col(2), col(3), col(5), col(6),
                           kr, col(4), zc(64), kr_rot, zc(96)], axis=-1).astype(BF16)
    assert win.shape[1] == EV_WIDTH
    wal = jnp.zeros((LANES, 512), F32)
    wal = wal.at[ROPE_B:ROPE_B + ALPHA_RANK, :256].set(w_alpha[0])
    wal = wal.at[ROPE_B + ALPHA_RANK:ROPE_B + 2 * ALPHA_RANK, 256:].set(w_alpha[1])
    bal = jnp.concatenate([b_alpha[0], b_alpha[1]])[None, :]
    wq = w_q_b.reshape(Q_RANK, H_B, NOPE_B + ROPE_B)
    qn, qr = wq[..., :NOPE_B], wq[..., NOPE_B:]
    pad = HEAD_SLOT - NOPE_B - ROPE_B
    zq = lambda w: jnp.zeros((Q_RANK, H_B, w), F32)
    wq_main = jnp.concatenate([qn, qr, zq(pad)], axis=-1).reshape(Q_RANK, MLA_W)
    wq_rot = jnp.concatenate([zq(NOPE_B), -qr[..., half:], qr[..., :half], zq(pad)], axis=-1).reshape(Q_RANK, MLA_W)
    wq2 = jnp.concatenate([wq_main, wq_rot], axis=-1).astype(BF16)
    wkv = w_kv_b.reshape(KV_RANK, H_B, NOPE_B + V_B)
    zk = lambda w: jnp.zeros((KV_RANK, H_B, w), F32)
    wk = jnp.concatenate([wkv[..., :NOPE_B], zk(HEAD_SLOT - NOPE_B)], axis=-1).reshape(KV_RANK, MLA_W).astype(BF16)
    wv = jnp.concatenate([wkv[..., NOPE_B:], zk(HEAD_SLOT - V_B)], axis=-1).reshape(KV_RANK, MLA_W).astype(BF16)
    e = np.zeros((LANES, H_B, HEAD_SLOT), np.float32)
    for r in range(ROPE_B):
        e[r, :, NOPE_B + r] = 1.0
    e = jnp.asarray(e.reshape(LANES, MLA_W), dtype=BF16)
    wo_a = w_out[:H_A * DV_A].astype(BF16)
    wo_b = jnp.concatenate([w_out[H_A * DV_A:].reshape(H_B, V_B, D),
                            jnp.zeros((H_B, HEAD_SLOT - V_B, D), F32)], axis=1).reshape(MLA_W, D).astype(BF16)
    return {"w_in": win, "w_alpha": wal.astype(BF16), "b_alpha": bal, "q_norm": q_norm[None, :], "w_q": wq2,
            "kv_norm": kv_norm[None, :], "w_k": wk, "w_v": wv, "e_rope": e, "gain": gla_gain[None, :],
            "wo_a": wo_a, "wo_b": wo_b}


def kernel(x_prompt, x_sample, c_prompt, c_sample, ada_w, ada_b, norm_pre, norm_post, ffn_w13, ffn_w2,
           ev_w_in, ev_w_out, gla_w_alpha, gla_b_alpha, gla_norm, mla_q_norm, mla_w_q_b, mla_kv_norm,
           mla_w_kv_b, od_w_in, od_w_out, sgu_norm, sgu_w_s, sgu_b):
    groups = [(x_prompt, 0), (x_sample, x_prompt.shape[0])]
    c_all = jnp.concatenate([c_prompt, c_sample], axis=0)
    mod5 = _ada(c_all, ada_w, ada_b).reshape(DEPTH, c_all.shape[0], 3, 3, D_MODEL)

    w13 = ffn_w13.astype(BF16)
    w2 = ffn_w2.astype(BF16)
    gla_consts = _gla_consts()
    rope = {x.shape[1]: _rope_tables(x.shape[1]) for x, _ in groups}
    fnet = {x.shape[1]: _fnet_consts(x.shape[1]) for x, _ in groups}
    even_w = [_even_weights(ev_w_in[i], ev_w_out[i], gla_w_alpha[i], gla_b_alpha[i], gla_norm[i], mla_q_norm[i],
                            mla_w_q_b[i], mla_kv_norm[i], mla_w_kv_b[i]) for i in range(ev_w_in.shape[0])]
    od_win = od_w_in.astype(BF16)
    od_wout = od_w_out.astype(BF16)
    sgu_ws = sgu_w_s.astype(BF16)
    sgu_bias = jnp.broadcast_to(sgu_b[..., None], sgu_b.shape + (DG_D,))

    outs = []
    for x, b_off in groups:
        S = x.shape[1]
        for l in range(DEPTH):
            pre = lambda s: norm_pre[l, s][None, :]
            post = lambda s: norm_post[l, s][None, :]
            ffn0 = _ffn_stage_spec(mod5, l, 0, 0, b_off, pre(0), post(0), w13, w2)
            ffn2 = _ffn_stage_spec(mod5, l, 2, 1, b_off, pre(2), post(2), w13, w2)
            i = l // 2
            if l % 2 == 0:
                wp = even_w[i]
                (x,) = _chain(x, [ffn0], CHAIN_TM, "ffn")
                qg, kg, vg, g, lg, q, k, vt = _chain(
                    x, [_even_in_stage_spec(mod5, l, b_off, pre(1), wp, rope[S])], CHAIN_TM, "even_in", write_x=False)
                o_f, o_b = _gla(qg, kg, vg, lg, gla_consts)
                ao = _attn(q, k, vt)
                (x,) = _chain(x, [_even_out_stage_spec(mod5, l, b_off, post(1), o_f, o_b, g, ao, wp), ffn2],
                              CHAIN_TM, "even_out_ffn")
            else:
                (x,) = _chain(x, [ffn0], CHAIN_TM, "ffn")
                zc, od = _chain(
                    x, [_odd_in_stage_spec(mod5, l, b_off, pre(1), od_win[i], sgu_norm[i][None, :], sgu_ws[i],
                                           sgu_bias[i])], CHAIN_TM, "odd_in", write_x=False)
                fc = _fnet(zc, fnet[S])
                (x,) = _chain(x, [_odd_out_stage_spec(mod5, l, b_off, post(1), fc, od, od_wout[i, :W_C],
                                                      od_wout[i, W_C:]), ffn2], CHAIN_TM, "odd_out_ffn")
        outs.append(x)
    return tuple(outs)
```

```python
import functools
import math

import jax
import jax.numpy as jnp
import numpy as np
from jax import lax
from jax.experimental import pallas as pl
from jax.experimental.pallas import tpu as pltpu

F32 = jnp.float32
BF16 = jnp.bfloat16

D_MODEL = 1024
DEPTH = 4
D_FF = 2816
EPS = 1e-6
H_A, DK_A, DV_A = 4, 64, 128
ALPHA_RANK = 16
GATE_NORM = 16.0
GLA_CHUNK = 64
H_B, Q_RANK, KV_RANK, NOPE_B, ROPE_B, V_B = 8, 256, 128, 64, 32, 64
ROPE_BASE = 10000.0
H_C, CG_C = 4, 128
H_D, DG_D = 4, 128
SGU_CHUNK = 128
HEAD_SLOT = 128
MLA_W = H_B * HEAD_SLOT
W_C = H_C * CG_C
W_D = H_D * DG_D

VMEM_LIMIT_BYTES = 56 * 1024 * 1024
LANES = 128

CHAIN_TM = 512
CHAIN_TM_EVEN_IN = 256
CHAIN_ROWS = 256
FFN_CHUNKS = (1024, 1024, 768)
GLA_TS = 256
GLA_STEP = 1024
ATT_TQ = 512
ATT_TK = 2048
ATT_HEADS = 2
ATT_UNROLL = 2
ATT_AHEAD = 1


def _cparams(sem):
    return pltpu.CompilerParams(dimension_semantics=sem, vmem_limit_bytes=VMEM_LIMIT_BYTES)


def _resident(shape):
    nd = len(shape)
    return pl.BlockSpec(shape, lambda *_: (0,) * nd, pipeline_mode=pl.Buffered(1))


def _rms(x, g):
    return x * lax.rsqrt(jnp.mean(x * x, axis=-1, keepdims=True) + EPS) * g


def _modulated(x, mod_ref, g_pre):
    shift = mod_ref[0:1, :]
    scale = mod_ref[1:2, :]
    return _rms(x, g_pre) * (1.0 + scale) + shift


def _residual(x, y, mod_ref, g_post, w_res):
    gate = mod_ref[2:3, :]
    return x + (w_res * (1.0 + gate)) * _rms(y, g_post)


def _silu(a):
    return a * jax.nn.sigmoid(a)


def _mod_spec(layer, sub, b_off):
    return pl.BlockSpec((None, None, None, 3, D_MODEL), lambda b, i: (layer, b_off + b, sub, 0, 0))


def _tok_spec(tm, width):
    return pl.BlockSpec((None, tm, width), lambda b, i: (b, i, 0))


def _ada_kernel(c_ref, w_ref, b_ref, o_ref):
    cc = _silu(c_ref[...]).astype(BF16)
    o_ref[...] = jnp.dot(cc, w_ref[...].astype(BF16), preferred_element_type=F32) + b_ref[...]


def _ada(c_all, ada_w, ada_b):
    bt = c_all.shape[0]
    n = ada_w.shape[-1]
    tn = n // 8
    return pl.pallas_call(
        _ada_kernel,
        grid=(DEPTH, n // tn),
        in_specs=[
            pl.BlockSpec((bt, D_MODEL), lambda l, j: (0, 0)),
            pl.BlockSpec((None, D_MODEL, tn), lambda l, j: (l, 0, j)),
            pl.BlockSpec((None, 1, tn), lambda l, j: (l, 0, j)),
        ],
        out_specs=pl.BlockSpec((None, bt, tn), lambda l, j: (l, 0, j)),
        out_shape=jax.ShapeDtypeStruct((DEPTH, bt, n), F32),
        compiler_params=_cparams(("arbitrary", "arbitrary")),
        name="ada_mod",
    )(c_all, ada_w, ada_b.reshape(DEPTH, 1, n))


def _ffn_stage(x, refs, outs, rs):
    mod_ref, gpre_ref, gpost_ref, w13_ref, w2_ref = refs
    h = _modulated(x, mod_ref, gpre_ref[...]).astype(BF16)
    acc = None
    off = 0
    for ck in FFN_CHUNKS:
        a = jnp.dot(h, w13_ref[:, off:off + ck], preferred_element_type=F32)
        b = jnp.dot(h, w13_ref[:, D_FF + off:D_FF + off + ck], preferred_element_type=F32)
        act = (_silu(a) * b).astype(BF16)
        part = jnp.dot(act, w2_ref[off:off + ck, :], preferred_element_type=F32)
        acc = part if acc is None else acc + part
        off += ck
    return _residual(x, acc, mod_ref, gpost_ref[...], 0.5)


EV_Q, EV_K, EV_V, EV_G, EV_CQ, EV_CKV, EV_A, EV_B = 0, 256, 512, 1024, 1536, 1792, 1920, 2048
EV_WIDTH = 2176


def _even_in_stage(x, refs, outs, rs):
    (mod_ref, gpre_ref, win_ref, walpha_ref, balpha_ref, qn_ref, wq_ref, kvn_ref, wk_ref, wv_ref, e_ref,
     cosq_ref, sinq_ref, cosk_ref, sink_ref) = refs
    qg_o, kg_o, vg_o, g_o, lg_o, q_o, k_o, vt_o = outs
    h = _modulated(x, mod_ref, gpre_ref[...]).astype(BF16)
    z = jnp.dot(h, win_ref[...], preferred_element_type=F32)
    yield
    qg_o[rs, :] = z[:, EV_Q:EV_K].astype(BF16)
    kg_o[rs, :] = z[:, EV_K:EV_V].astype(BF16)
    vg_o[rs, :] = z[:, EV_V:EV_G].astype(BF16)
    g_o[rs, :] = z[:, EV_G:EV_CQ]
    cq = z[:, EV_CQ:EV_CKV]
    ckv = z[:, EV_CKV:EV_A]
    blk_a = z[:, EV_A:EV_B]
    blk_b = z[:, EV_B:EV_WIDTH]
    al = jnp.dot(blk_a.astype(BF16), walpha_ref[...], preferred_element_type=F32) + balpha_ref[...]
    cqn = _rms(cq, qn_ref[...]).astype(BF16)
    q2 = jnp.dot(cqn, wq_ref[...], preferred_element_type=F32)
    ckvn = _rms(ckv, kvn_ref[...]).astype(BF16)
    krr = (blk_a * cosk_ref[rs, :] + blk_b * sink_ref[rs, :]).astype(BF16)
    kk = (jnp.dot(ckvn, wk_ref[...], preferred_element_type=F32)
          + jnp.dot(krr, e_ref[...], preferred_element_type=F32))
    vv = jnp.dot(ckvn, wv_ref[...], preferred_element_type=F32)
    yield
    lg_o[rs, :] = (jnp.minimum(al, 0.0) - jnp.log1p(jnp.exp(-jnp.abs(al)))) * (1.0 / GATE_NORM)
    cosq = cosq_ref[rs, :]
    sinq = sinq_ref[rs, :]
    scale = (NOPE_B + ROPE_B) ** -0.5 * math.log2(math.e)
    for hh in range(H_B):
        lo = hh * HEAD_SLOT
        qs = q2[:, lo:lo + HEAD_SLOT] * cosq + q2[:, MLA_W + lo:MLA_W + lo + HEAD_SLOT] * sinq
        q_o[rs, lo:lo + HEAD_SLOT] = (qs * scale).astype(BF16)
    k_o[rs, :] = kk.astype(BF16)
    vlane = lax.broadcasted_iota(jnp.int32, (1, MLA_W), 1) & (HEAD_SLOT - 1)
    ones_lane = jnp.where(vlane == V_B, 1.0, 0.0)
    vt_o[:, rs] = (vv + ones_lane).T.astype(BF16)
    return x


def _even_out_stage(x, refs, outs, rs):
    mod_ref, gpost_ref, of_ref, ob_ref, g_ref, gain_ref, ao_ref, wa_ref, wb_ref = refs
    o = of_ref[rs, :] + ob_ref[rs, :]
    g = g_ref[rs, :]
    gain = gain_ref[...]
    parts = []
    for hh in range(H_A):
        sl = slice(hh * DV_A, (hh + 1) * DV_A)
        parts.append(_rms(o[:, sl], gain) * _silu(g[:, sl]))
    oa = jnp.concatenate(parts, axis=-1).astype(BF16)
    mix = (jnp.dot(oa, wa_ref[...], preferred_element_type=F32)
           + jnp.dot(ao_ref[rs, :], wb_ref[...], preferred_element_type=F32))
    return _residual(x, mix, mod_ref, gpost_ref[...], 1.0)


def _odd_in_stage(x, refs, outs, rs):
    mod_ref, gpre_ref, win_ref, sgun_ref, ws_ref, bs_ref = refs
    zc_o, od_o = outs
    h = _modulated(x, mod_ref, gpre_ref[...]).astype(BF16)
    z = jnp.dot(h, win_ref[...], preferred_element_type=F32)
    yield
    zc_o[rs, :] = z[:, :W_C].astype(BF16)
    zd = z[:, W_C:]
    ge = 0.5 * zd * (1.0 + lax.erf(zd * (2.0 ** -0.5)))
    u = ge[:, :W_D]
    vn = _rms(ge[:, W_D:], sgun_ref[...]).astype(BF16)
    tiles = [(slice(c * SGU_CHUNK, (c + 1) * SGU_CHUNK), hh, slice(hh * DG_D, (hh + 1) * DG_D))
             for c in range(x.shape[0] // SGU_CHUNK) for hh in range(H_D)]
    svs = [jnp.dot(ws_ref[hh], vn[cr, cs], preferred_element_type=F32) for cr, hh, cs in tiles]
    yield
    for (cr, hh, cs), sv in zip(tiles, svs):
        orow = slice(rs.start + cr.start, rs.start + cr.stop)
        od_o[orow, cs] = (u[cr, cs] * (sv + bs_ref[hh])).astype(BF16)
    return x


def _odd_out_stage(x, refs, outs, rs):
    mod_ref, gpost_ref, fc_ref, od_ref, wa_ref, wb_ref = refs
    mix = (jnp.dot(fc_ref[rs, :], wa_ref[...], preferred_element_type=F32)
           + jnp.dot(od_ref[rs, :], wb_ref[...], preferred_element_type=F32))
    return _residual(x, mix, mod_ref, gpost_ref[...], 1.0)


def _chain_kernel(*refs, fns, counts, rows, write_x):
    n_in = 1 + sum(c[0] for c in counts)
    x_ref = refs[0]
    stage_refs = []
    i, o = 1, n_in + int(write_x)
    for n_args, n_outs in counts:
        stage_refs.append((refs[i:i + n_args], refs[o:o + n_outs]))
        i += n_args
        o += n_outs
    groups = [slice(r * rows, (r + 1) * rows) for r in range(x_ref.shape[0] // rows)]
    xs = [x_ref[rs, :] for rs in groups]
    for fn, (ins, outs) in zip(fns, stage_refs):
        xs = _lockstep([fn(x, ins, outs, rs) for x, rs in zip(xs, groups)])
    if write_x:
        for x, rs in zip(xs, groups):
            refs[n_in][rs, :] = x


def _lockstep(results):
    out = list(results)
    live = [i for i, r in enumerate(out) if hasattr(r, "send")]
    while live:
        for i in list(live):
            try:
                next(out[i])
            except StopIteration as done:
                out[i] = done.value
                live.remove(i)
    return out


def _chain(x, stages, tm, name, write_x=True):
    B, S, D = x.shape
    tm = min(tm, S)
    arrays, specs = [x], [_tok_spec(tm, D)]
    out_shapes = [jax.ShapeDtypeStruct((B, S, D), F32)] if write_x else []
    out_specs = [_tok_spec(tm, D)] if write_x else []
    counts = []
    for _, ins, outs in stages:
        arrays += [a for a, _ in ins]
        specs += [s(tm) for _, s in ins]
        for w, dt, *transposed in outs:
            if transposed:
                out_shapes.append(jax.ShapeDtypeStruct((B, w, S), dt))
                out_specs.append(pl.BlockSpec((None, w, tm), lambda b, i: (b, 0, i)))
            else:
                out_shapes.append(jax.ShapeDtypeStruct((B, S, w), dt))
                out_specs.append(_tok_spec(tm, w))
        counts.append((len(ins), len(outs)))
    return pl.pallas_call(
        functools.partial(_chain_kernel, fns=tuple(s[0] for s in stages), counts=tuple(counts),
                          rows=min(CHAIN_ROWS, tm), write_x=write_x),
        grid=(B, S // tm),
        in_specs=specs,
        out_specs=out_specs,
        out_shape=out_shapes,
        compiler_params=_cparams(("arbitrary", "arbitrary")),
        name=name,
    )(*arrays)


def _res(a):
    return (a, lambda tm: _resident(a.shape))


def _tok(a):
    return (a, lambda tm: _tok_spec(tm, a.shape[-1]))


def _pos(a):
    return (a, lambda tm: pl.BlockSpec((tm, a.shape[-1]), lambda b, i: (i, 0)))


def _ffn_stage_spec(mod5, layer, sub, which, b_off, g_pre, g_post, w13, w2):
    wspec = lambda r, c: (lambda tm: pl.BlockSpec((None, None, r, c), lambda b, i: (layer, which, 0, 0),
                                                  pipeline_mode=pl.Buffered(1)))
    ins = [(mod5, lambda tm: _mod_spec(layer, sub, b_off)), _res(g_pre), _res(g_post),
           (w13, wspec(D_MODEL, 2 * D_FF)), (w2, wspec(D_FF, D_MODEL))]
    return (_ffn_stage, ins, [])


def _even_in_stage_spec(mod5, layer, b_off, g_pre, wp, tabs):
    ins = [(mod5, lambda tm: _mod_spec(layer, 1, b_off)), _res(g_pre), _res(wp["w_in"]), _res(wp["w_alpha"]),
           _res(wp["b_alpha"]), _res(wp["q_norm"]), _res(wp["w_q"]), _res(wp["kv_norm"]), _res(wp["w_k"]),
           _res(wp["w_v"]), _res(wp["e_rope"]),
           _pos(tabs["cosq"]), _pos(tabs["sinq"]), _pos(tabs["cosk"]), _pos(tabs["sink"])]
    outs = [(256, BF16), (256, BF16), (512, BF16), (512, F32), (512, F32), (MLA_W, BF16), (MLA_W, BF16),
            (MLA_W, BF16, True)]
    return (_even_in_stage, ins, outs)


def _even_out_stage_spec(mod5, layer, b_off, g_post, o_f, o_b, g, ao, wp):
    ins = [(mod5, lambda tm: _mod_spec(layer, 1, b_off)), _res(g_post), _tok(o_f), _tok(o_b), _tok(g),
           _res(wp["gain"]), _tok(ao), _res(wp["wo_a"]), _res(wp["wo_b"])]
    return (_even_out_stage, ins, [])


def _odd_in_stage_spec(mod5, layer, b_off, g_pre, w_in, sgu_norm, w_s, b_s):
    ins = [(mod5, lambda tm: _mod_spec(layer, 1, b_off)), _res(g_pre), _res(w_in), _res(sgu_norm), _res(w_s), _res(b_s)]
    return (_odd_in_stage, ins, [(W_C, BF16), (W_D, BF16)])


def _odd_out_stage_spec(mod5, layer, b_off, g_post, fc, od, wa, wb):
    ins = [(mod5, lambda tm: _mod_spec(layer, 1, b_off)), _res(g_post), _tok(fc), _tok(od), _res(wa), _res(wb)]
    return (_odd_out_stage, ins, [])


def _gla_kernel(qf_ref, kf_ref, lgf_ref, vf_ref, qb_ref, kb_ref, lgb_ref, vb_ref,
                cumf_ref, cumb_ref, of_ref, ob_ref, sf_ref, sb_ref):
    @pl.when(pl.program_id(2) == 0)
    def _():
        sf_ref[...] = jnp.zeros_like(sf_ref)
        sb_ref[...] = jnp.zeros_like(sb_ref)

    ts = GLA_TS
    nchunk = ts // GLA_CHUNK
    nblk = qf_ref.shape[0] // ts
    r = lax.broadcasted_iota(jnp.int32, (ts, ts), 0)
    c = lax.broadcasted_iota(jnp.int32, (ts, ts), 1)
    same = (r >> 6) == (c >> 6)
    lane = lax.broadcasted_iota(jnp.int32, (ts, 2 * DK_A), 1)
    row = lax.broadcasted_iota(jnp.int32, (ts, 2 * DV_A), 0)
    srow = lax.broadcasted_iota(jnp.int32, (2 * DK_A, 2 * DV_A), 0)
    scol = lax.broadcasted_iota(jnp.int32, (2 * DK_A, 2 * DV_A), 1)
    own = (srow >= DK_A) == (scol >= DV_A)
    scale = DK_A ** -0.5

    class Inst:
        pass

    insts = []
    for i in range(nblk):
        for rev in (False, True):
            it = Inst()
            it.rev = rev
            blk = (nblk - 1 - i) if rev else i
            it.rows = slice(blk * ts, (blk + 1) * ts)
            it.q_ref, it.k_ref, it.lg_ref, it.v_ref = (qb_ref, kb_ref, lgb_ref, vb_ref) if rev else \
                                                      (qf_ref, kf_ref, lgf_ref, vf_ref)
            it.cum_ref = cumb_ref if rev else cumf_ref
            it.amask = same & ((c > r) if rev else (c <= r))
            it.edge = 0 if rev else GLA_CHUNK - 1
            it.o_ref = ob_ref if rev else of_ref
            insts.append(it)

    for it in insts:
        it.lg = it.lg_ref[it.rows, :]
        hi = it.lg.astype(BF16)
        lo = (it.lg - hi.astype(F32)).astype(BF16)
        cum = it.cum_ref[...]
        it.b = jnp.dot(cum, hi, preferred_element_type=F32) + jnp.dot(cum, lo, preferred_element_type=F32)
    for it in insts:
        q = it.q_ref[it.rows, :].astype(F32) * scale
        k = it.k_ref[it.rows, :].astype(F32)
        b = it.b
        tot = jnp.concatenate(
            [jnp.broadcast_to(b[n * GLA_CHUNK + it.edge:n * GLA_CHUNK + it.edge + 1, :], (GLA_CHUNK, 2 * DK_A))
             for n in range(nchunk)], axis=0)
        qt = q * jnp.exp((b - it.lg) if it.rev else b)
        kt = (k * jnp.exp(-b)).astype(BF16)
        it.kdt = (k * jnp.exp(tot - b)).T.astype(BF16)
        it.bt = b.T
        it.qtb = qt.astype(BF16)
        it.a = []
        for hh in range(2):
            qh = jnp.where((lane >= hh * DK_A) & (lane < (hh + 1) * DK_A), qt, 0.0).astype(BF16)
            it.a.append(lax.dot_general(qh, kt, (((1,), (1,)), ((), ())), preferred_element_type=F32))
    for it in insts:
        v = it.v_ref[it.rows, :]
        it.intra = jnp.concatenate(
            [jnp.dot(jnp.where(it.amask, it.a[hh], 0.0).astype(BF16), v[:, hh * DV_A:(hh + 1) * DV_A],
                     preferred_element_type=F32) for hh in range(2)], axis=-1)
        it.kv = []
        for n in range(nchunk):
            r0 = n * GLA_CHUNK
            vn = jnp.where((row >= r0) & (row < r0 + GLA_CHUNK), v, jnp.zeros_like(v))
            it.kv.append(jnp.dot(it.kdt, vn, preferred_element_type=F32))
    state = {False: sf_ref[...], True: sb_ref[...]}
    for it in insts:
        it.inter = [None] * nchunk
    for i in range(nblk):
        pair = insts[2 * i:2 * i + 2]
        for step in range(nchunk):
            for it in pair:
                n = (nchunk - 1 - step) if it.rev else step
                r0 = n * GLA_CHUNK
                st = state[it.rev]
                it.inter[n] = jnp.dot(it.qtb[r0:r0 + GLA_CHUNK, :], st.astype(BF16), preferred_element_type=F32)
                dec = jnp.exp(it.bt[:, r0 + it.edge:r0 + it.edge + 1])
                state[it.rev] = jnp.where(own, dec * st + it.kv[n], 0.0)
        for it in pair:
            it.o_ref[it.rows, :] = it.intra + jnp.concatenate(it.inter, axis=0)
    sf_ref[...] = state[False]
    sb_ref[...] = state[True]


def _gla(qg, kg, vg, lg, consts):
    B, S, _ = qg.shape
    ts = min(GLA_STEP, S)
    nt = S // ts
    fwd = lambda w, off: pl.BlockSpec((None, ts, w), lambda b, p, t: (b, t, p + off))
    bwd = lambda w, off: pl.BlockSpec((None, ts, w), lambda b, p, t: (b, nt - 1 - t, p + off))
    npair = H_A // 2
    return pl.pallas_call(
        _gla_kernel,
        grid=(B, npair, nt),
        in_specs=[
            fwd(128, 0), fwd(128, 0), fwd(128, 0), fwd(256, 0),
            bwd(128, 0), bwd(128, 0), bwd(128, npair), bwd(256, 0),
            _resident((GLA_TS, GLA_TS)), _resident((GLA_TS, GLA_TS)),
        ],
        out_specs=[fwd(256, 0), bwd(256, 0)],
        out_shape=[jax.ShapeDtypeStruct((B, S, H_A * DV_A), F32)] * 2,
        scratch_shapes=[pltpu.VMEM((2 * DK_A, 2 * DV_A), F32)] * 2,
        compiler_params=_cparams(("arbitrary", "arbitrary", "arbitrary")),
        name="gla",
    )(qg, kg, lg, vg, qg, kg, lg, vg, consts["cum_f"], consts["cum_b"])


def _attn_kernel(q_ref, k_ref, vt_ref, o_ref, *, tk, nk):
    tq = q_ref.shape[0]
    slots = [slice(h * HEAD_SLOT, (h + 1) * HEAD_SLOT) for h in range(ATT_HEADS)]
    qts = [q_ref[:, sl].astype(F32).T.astype(BF16) for sl in slots]

    group = min(ATT_UNROLL, nk)

    def scores(h, start):
        return jnp.dot(k_ref[pl.ds(start, tk), slots[h]], qts[h], preferred_element_type=F32)

    def absorb(h, start, st, state):
        m, acc = state
        m_new = jnp.maximum(m, jnp.max(st, axis=0, keepdims=True))
        pt = jnp.exp2(st - m_new).astype(BF16)
        acc = jnp.exp2(m - m_new) * acc + jnp.dot(vt_ref[slots[h], pl.ds(start, tk)], pt,
                                                  preferred_element_type=F32)
        return m_new, acc

    def body(j, carry):
        items = [(pl.multiple_of((j * group + g) * tk, tk), h) for g in range(group) for h in range(ATT_HEADS)]
        state = list(carry)
        pending = [scores(h, start) for start, h in items[:ATT_AHEAD]]
        for i, (start, h) in enumerate(items):
            if i + ATT_AHEAD < len(items):
                nstart, nh = items[i + ATT_AHEAD]
                pending.append(scores(nh, nstart))
            state[h] = absorb(h, start, pending.pop(0), state[h])
        return tuple(state)

    init = tuple((jnp.full((1, tq), -jnp.inf, F32), jnp.zeros((HEAD_SLOT, tq), F32)) for _ in slots)
    res = lax.fori_loop(0, nk // group, body, init)
    for (_, acc), sl in zip(res, slots):
        o_ref[:, sl] = (acc / acc[V_B:V_B + 1, :]).T.astype(BF16)


def _attn(q, k, vt):
    B, S, _ = q.shape
    tq = min(ATT_TQ, S)
    tk = min(ATT_TK, S)
    w = ATT_HEADS * HEAD_SLOT
    return pl.pallas_call(
        functools.partial(_attn_kernel, tk=tk, nk=S // tk),
        grid=(B, H_B // ATT_HEADS, S // tq),
        in_specs=[
            pl.BlockSpec((None, tq, w), lambda b, h, i: (b, i, h)),
            pl.BlockSpec((None, S, w), lambda b, h, i: (b, 0, h), pipeline_mode=pl.Buffered(1)),
            pl.BlockSpec((None, w, S), lambda b, h, i: (b, h, 0), pipeline_mode=pl.Buffered(1)),
        ],
        out_specs=pl.BlockSpec((None, tq, w), lambda b, h, i: (b, i, h)),
        out_shape=jax.ShapeDtypeStruct((B, S, MLA_W), BF16),
        compiler_params=_cparams(("arbitrary", "arbitrary", "arbitrary")),
        name="mla_attn",
    )(q, k, vt)


def _fnet1_kernel(f_ref, z_ref, a_ref):
    a_ref[...] = jnp.dot(f_ref[...], z_ref[...], preferred_element_type=F32).astype(BF16)


def _fnet2_kernel(a_ref, cs_ref, g_ref, o_ref, *, kt, n2, norm):
    prods = []
    for j in range(kt):
        a2 = a_ref[:, j].reshape(2 * n2, W_C)
        prods.append([jnp.dot(a2[:, gg * CG_C:(gg + 1) * CG_C], cs_ref[...], preferred_element_type=F32)
                      for gg in range(H_C)])
    for j in range(kt):
        br = [p[:n2, :CG_C] + p[n2:, CG_C:] for p in prods[j]]
        bi = [p[n2:, :CG_C] - p[:n2, CG_C:] for p in prods[j]]
        b2 = jnp.concatenate([jnp.concatenate(br, axis=-1), jnp.concatenate(bi, axis=-1)], axis=0).astype(BF16)
        zz = jnp.dot(g_ref[j], b2, preferred_element_type=F32)
        o_ref[:, j * W_C:(j + 1) * W_C] = (zz * norm).astype(BF16)


def _fnet(zc, consts):
    B, S, wc = zc.shape
    n1, n2 = consts["n1"], consts["n2"]
    tc = min(n2 * wc, 8192)
    a = pl.pallas_call(
        _fnet1_kernel,
        grid=(B, n2 * wc // tc),
        in_specs=[_resident((2 * n1, n1)), pl.BlockSpec((None, n1, tc), lambda b, j: (b, 0, j))],
        out_specs=pl.BlockSpec((None, 2 * n1, tc), lambda b, j: (b, 0, j)),
        out_shape=jax.ShapeDtypeStruct((B, 2 * n1, n2 * wc), BF16),
        compiler_params=_cparams(("arbitrary", "arbitrary")),
        name="fnet_stage1",
    )(consts["f1"], zc.reshape(B, n1, n2 * wc))
    kt = 8
    out = pl.pallas_call(
        functools.partial(_fnet2_kernel, kt=kt, n2=n2, norm=float(1.0 / math.sqrt(S * CG_C))),
        grid=(B, n1 // kt),
        in_specs=[
            pl.BlockSpec((None, 2, kt, n2, wc), lambda b, j: (b, 0, j, 0, 0)),
            _resident((CG_C, 2 * CG_C)),
            pl.BlockSpec((kt, n2, 2 * n2), lambda b, j: (j, 0, 0)),
        ],
        out_specs=pl.BlockSpec((None, n2, kt * wc), lambda b, j: (b, 0, j)),
        out_shape=jax.ShapeDtypeStruct((B, n2, n1 * wc), BF16),
        compiler_params=_cparams(("arbitrary", "arbitrary")),
        name="fnet_stage2",
    )(a.reshape(B, 2, n1, n2, wc), consts["cs"], consts["g"])
    return out.reshape(B, S, wc)


def _gla_consts():
    ts = GLA_TS
    r = np.arange(ts)[:, None]
    c = np.arange(ts)[None, :]
    same = (r // GLA_CHUNK) == (c // GLA_CHUNK)
    as_bf16 = lambda m: jnp.asarray(m.astype(np.float32), dtype=BF16)
    return {"cum_f": as_bf16(same & (c <= r)), "cum_b": as_bf16(same & (c >= r))}


def _rope_tables(S):
    half = ROPE_B // 2
    inv = ROPE_BASE ** (-jnp.arange(half, dtype=F32) / half)
    ang = jnp.arange(S, dtype=F32)[:, None] * inv[None, :]
    cos, sin = jnp.cos(ang), jnp.sin(ang)
    cc = jnp.concatenate([cos, cos], axis=-1)
    ss = jnp.concatenate([sin, sin], axis=-1)
    z = lambda w: jnp.zeros((S, w), F32)
    return {
        "cosq": jnp.concatenate([jnp.ones((S, NOPE_B), F32), cc, z(HEAD_SLOT - NOPE_B - ROPE_B)], axis=-1),
        "sinq": jnp.concatenate([z(NOPE_B), ss, z(HEAD_SLOT - NOPE_B - ROPE_B)], axis=-1),
        "cosk": jnp.concatenate([cc, z(LANES - ROPE_B)], axis=-1),
        "sink": jnp.concatenate([ss, z(LANES - ROPE_B)], axis=-1),
    }


def _fnet_consts(S):
    n1 = 1 << (int(math.log2(S)) // 2)
    n2 = S // n1
    two_pi = 2.0 * math.pi
    k1 = jnp.arange(n1, dtype=jnp.int32)
    ph1 = ((k1[:, None] * k1[None, :]) % n1).astype(F32) * (two_pi / n1)
    f1 = jnp.concatenate([jnp.cos(ph1), -jnp.sin(ph1)], axis=0).astype(BF16)
    ch = jnp.arange(CG_C, dtype=jnp.int32)
    phc = ((ch[:, None] * ch[None, :]) % CG_C).astype(F32) * (two_pi / CG_C)
    cs = jnp.concatenate([jnp.cos(phc), jnp.sin(phc)], axis=-1).astype(BF16)
    s2 = jnp.arange(n2, dtype=jnp.int32)
    k = k1[:, None, None] + n1 * s2[None, :, None]
    ph = ((k * s2[None, None, :]) % S).astype(F32) * (two_pi / S)
    g = jnp.concatenate([jnp.cos(ph), jnp.sin(ph)], axis=-1).astype(BF16)
    return {"n1": n1, "n2": n2, "f1": f1, "cs": cs, "g": g}


def _even_weights(w_in, w_out, w_alpha, b_alpha, gla_gain, q_norm, w_q_b, kv_norm, w_kv_b):
    D = D_MODEL
    offs = np.cumsum([0, 256, 256, 512, 512, 2 * ALPHA_RANK, Q_RANK, KV_RANK, ROPE_B])
    col = lambda i: w_in[:, offs[i]:offs[i + 1]]
    kr = col(7)
    half = ROPE_B // 2
    kr_rot = jnp.concatenate([-kr[:, half:], kr[:, :half]], axis=-1)
    zc = lambda w: jnp.zeros((D, w), F32)
    win = jnp.concatenate([col(0), col(1), col(2), col(3), col(5), col(6),
                           kr, col(4), zc(64), kr_rot, zc(96)], axis=-1).astype(BF16)
    assert win.shape[1] == EV_WIDTH
    wal = jnp.zeros((LANES, 512), F32)
    wal = wal.at[ROPE_B:ROPE_B + ALPHA_RANK, :256].set(w_alpha[0])
    wal = wal.at[ROPE_B + ALPHA_RANK:ROPE_B + 2 * ALPHA_RANK, 256:].set(w_alpha[1])
    bal = jnp.concatenate([b_alpha[0], b_alpha[1]])[None, :]
    wq = w_q_b.reshape(Q_RANK, H_B, NOPE_B + ROPE_B)
    qn, qr = wq[..., :NOPE_B], wq[..., NOPE_B:]
    pad = HEAD_SLOT - NOPE_B - ROPE_B
    zq = lambda w: jnp.zeros((Q_RANK, H_B, w), F32)
    wq_main = jnp.concatenate([qn, qr, zq(pad)], axis=-1).reshape(Q_RANK, MLA_W)
    wq_rot = jnp.concatenate([zq(NOPE_B), -qr[..., half:], qr[..., :half], zq(pad)], axis=-1).reshape(Q_RANK, MLA_W)
    wq2 = jnp.concatenate([wq_main, wq_rot], axis=-1).astype(BF16)
    wkv = w_kv_b.reshape(KV_RANK, H_B, NOPE_B + V_B)
    zk = lambda w: jnp.zeros((KV_RANK, H_B, w), F32)
    wk = jnp.concatenate([wkv[..., :NOPE_B], zk(HEAD_SLOT - NOPE_B)], axis=-1).reshape(KV_RANK, MLA_W).astype(BF16)
    wv = jnp.concatenate([wkv[..., NOPE_B:], zk(HEAD_SLOT - V_B)], axis=-1).reshape(KV_RANK, MLA_W).astype(BF16)
    e = np.zeros((LANES, H_B, HEAD_SLOT), np.float32)
    for r in range(ROPE_B):
        e[r, :, NOPE_B + r] = 1.0
    e = jnp.asarray(e.reshape(LANES, MLA_W), dtype=BF16)
    wo_a = w_out[:H_A * DV_A].astype(BF16)
    wo_b = jnp.concatenate([w_out[H_A * DV_A:].reshape(H_B, V_B, D),
                            jnp.zeros((H_B, HEAD_SLOT - V_B, D), F32)], axis=1).reshape(MLA_W, D).astype(BF16)
    return {"w_in": win, "w_alpha": wal.astype(BF16), "b_alpha": bal, "q_norm": q_norm[None, :], "w_q": wq2,
            "kv_norm": kv_norm[None, :], "w_k": wk, "w_v": wv, "e_rope": e, "gain": gla_gain[None, :],
            "wo_a": wo_a, "wo_b": wo_b}


def kernel(x_prompt, x_sample, c_prompt, c_sample, ada_w, ada_b, norm_pre, norm_post, ffn_w13, ffn_w2,
           ev_w_in, ev_w_out, gla_w_alpha, gla_b_alpha, gla_norm, mla_q_norm, mla_w_q_b, mla_kv_norm,
           mla_w_kv_b, od_w_in, od_w_out, sgu_norm, sgu_w_s, sgu_b):
    groups = [(x_prompt, 0), (x_sample, x_prompt.shape[0])]
    c_all = jnp.concatenate([c_prompt, c_sample], axis=0)
    mod5 = _ada(c_all, ada_w, ada_b).reshape(DEPTH, c_all.shape[0], 3, 3, D_MODEL)

    w13 = ffn_w13.astype(BF16)
    w2 = ffn_w2.astype(BF16)
    gla_consts = _gla_consts()
    rope = {x.shape[1]: _rope_tables(x.shape[1]) for x, _ in groups}
    fnet = {x.shape[1]: _fnet_consts(x.shape[1]) for x, _ in groups}
    even_w = [_even_weights(ev_w_in[i], ev_w_out[i], gla_w_alpha[i], gla_b_alpha[i], gla_norm[i], mla_q_norm[i],
                            mla_w_q_b[i], mla_kv_norm[i], mla_w_kv_b[i]) for i in range(ev_w_in.shape[0])]
    od_win = od_w_in.astype(BF16)
    od_wout = od_w_out.astype(BF16)
    sgu_ws = sgu_w_s.astype(BF16)
    sgu_bias = jnp.broadcast_to(sgu_b[..., None], sgu_b.shape + (DG_D,))

    outs = []
    for x, b_off in groups:
        S = x.shape[1]
        for l in range(DEPTH):
            pre = lambda s: norm_pre[l, s][None, :]
            post = lambda s: norm_post[l, s][None, :]
            ffn0 = _ffn_stage_spec(mod5, l, 0, 0, b_off, pre(0), post(0), w13, w2)
            ffn2 = _ffn_stage_spec(mod5, l, 2, 1, b_off, pre(2), post(2), w13, w2)
            i = l // 2
            if l % 2 == 0:
                wp = even_w[i]
                (x,) = _chain(x, [ffn0], CHAIN_TM, "ffn")
                qg, kg, vg, g, lg, q, k, vt = _chain(
                    x, [_even_in_stage_spec(mod5, l, b_off, pre(1), wp, rope[S])], CHAIN_TM, "even_in", write_x=False)
                o_f, o_b = _gla(qg, kg, vg, lg, gla_consts)
                ao = _attn(q, k, vt)
                (x,) = _chain(x, [_even_out_stage_spec(mod5, l, b_off, post(1), o_f, o_b, g, ao, wp), ffn2],
                              CHAIN_TM, "even_out_ffn")
            else:
                (x,) = _chain(x, [ffn0], CHAIN_TM, "ffn")
                zc, od = _chain(
                    x, [_odd_in_stage_spec(mod5, l, b_off, pre(1), od_win[i], sgu_norm[i][None, :], sgu_ws[i],
                                           sgu_bias[i])], CHAIN_TM, "odd_in", write_x=False)
                fc = _fnet(zc, fnet[S])
                (x,) = _chain(x, [_odd_out_stage_spec(mod5, l, b_off, post(1), fc, od, od_wout[i, :W_C],
                                                      od_wout[i, W_C:]), ffn2], CHAIN_TM, "odd_out_ffn")
        outs.append(x)
    return tuple(outs)
```

```python
import functools
import math

import jax
import jax.numpy as jnp
import numpy as np
from jax import lax
from jax.experimental import pallas as pl
from jax.experimental.pallas import tpu as pltpu

F32 = jnp.float32
BF16 = jnp.bfloat16

D_MODEL = 1024
DEPTH = 4
D_FF = 2816
EPS = 1e-6
H_A, DK_A, DV_A = 4, 64, 128
ALPHA_RANK = 16
GATE_NORM = 16.0
GLA_CHUNK = 64
H_B, Q_RANK, KV_RANK, NOPE_B, ROPE_B, V_B = 8, 256, 128, 64, 32, 64
ROPE_BASE = 10000.0
H_C, CG_C = 4, 128
H_D, DG_D = 4, 128
SGU_CHUNK = 128
HEAD_SLOT = 128
MLA_W = H_B * HEAD_SLOT
W_C = H_C * CG_C
W_D = H_D * DG_D

VMEM_LIMIT_BYTES = 56 * 1024 * 1024
LANES = 128

CHAIN_TM = 512
CHAIN_TM_EVEN_IN = 256
CHAIN_ROWS = 256
FFN_CHUNKS = (1024, 1024, 768)
GLA_TS = 256
GLA_STEP = 2048
ATT_TQ = 1024
ATT_TK = 2048
ATT_HEADS = 2
ATT_UNROLL = 2
ATT_AHEAD = 1


def _cparams(sem):
    return pltpu.CompilerParams(dimension_semantics=sem, vmem_limit_bytes=VMEM_LIMIT_BYTES)


def _resident(shape):
    nd = len(shape)
    return pl.BlockSpec(shape, lambda *_: (0,) * nd, pipeline_mode=pl.Buffered(1))


def _rms(x, g):
    return x * lax.rsqrt(jnp.mean(x * x, axis=-1, keepdims=True) + EPS) * g


def _modulated(x, mod_ref, g_pre):
    shift = mod_ref[0:1, :]
    scale = mod_ref[1:2, :]
    return _rms(x, g_pre) * (1.0 + scale) + shift


def _residual(x, y, mod_ref, g_post, w_res):
    gate = mod_ref[2:3, :]
    return x + (w_res * (1.0 + gate)) * _rms(y, g_post)


def _silu(a):
    return a * jax.nn.sigmoid(a)


def _mod_spec(layer, sub, b_off):
    return pl.BlockSpec((None, None, None, 3, D_MODEL), lambda b, i: (layer, b_off + b, sub, 0, 0))


def _tok_spec(tm, width):
    return pl.BlockSpec((None, tm, width), lambda b, i: (b, i, 0))


def _ada_kernel(c_ref, w_ref, b_ref, o_ref):
    cc = _silu(c_ref[...]).astype(BF16)
    o_ref[...] = jnp.dot(cc, w_ref[...].astype(BF16), preferred_element_type=F32) + b_ref[...]


def _ada(c_all, ada_w, ada_b):
    bt = c_all.shape[0]
    n = ada_w.shape[-1]
    tn = n // 8
    return pl.pallas_call(
        _ada_kernel,
        grid=(DEPTH, n // tn),
        in_specs=[
            pl.BlockSpec((bt, D_MODEL), lambda l, j: (0, 0)),
            pl.BlockSpec((None, D_MODEL, tn), lambda l, j: (l, 0, j)),
            pl.BlockSpec((None, 1, tn), lambda l, j: (l, 0, j)),
        ],
        out_specs=pl.BlockSpec((None, bt, tn), lambda l, j: (l, 0, j)),
        out_shape=jax.ShapeDtypeStruct((DEPTH, bt, n), F32),
        compiler_params=_cparams(("arbitrary", "arbitrary")),
        name="ada_mod",
    )(c_all, ada_w, ada_b.reshape(DEPTH, 1, n))


def _ffn_stage(x, refs, outs, rs):
    mod_ref, gpre_ref, gpost_ref, w13_ref, w2_ref = refs
    h = _modulated(x, mod_ref, gpre_ref[...]).astype(BF16)
    acc = None
    off = 0
    for ck in FFN_CHUNKS:
        a = jnp.dot(h, w13_ref[:, off:off + ck], preferred_element_type=F32)
        b = jnp.dot(h, w13_ref[:, D_FF + off:D_FF + off + ck], preferred_element_type=F32)
        act = (_silu(a) * b).astype(BF16)
        part = jnp.dot(act, w2_ref[off:off + ck, :], preferred_element_type=F32)
        acc = part if acc is None else acc + part
        off += ck
    return _residual(x, acc, mod_ref, gpost_ref[...], 0.5)


EV_Q, EV_K, EV_V, EV_G, EV_CQ, EV_CKV, EV_A, EV_B = 0, 256, 512, 1024, 1536, 1792, 1920, 2048
EV_WIDTH = 2176


def _even_in_stage(x, refs, outs, rs):
    (mod_ref, gpre_ref, win_ref, walpha_ref, balpha_ref, qn_ref, wq_ref, kvn_ref, wk_ref, wv_ref, e_ref,
     cosq_ref, sinq_ref, cosk_ref, sink_ref) = refs
    qg_o, kg_o, vg_o, g_o, lg_o, q_o, k_o, vt_o = outs
    h = _modulated(x, mod_ref, gpre_ref[...]).astype(BF16)
    z = jnp.dot(h, win_ref[...], preferred_element_type=F32)
    yield
    qg_o[rs, :] = z[:, EV_Q:EV_K].astype(BF16)
    kg_o[rs, :] = z[:, EV_K:EV_V].astype(BF16)
    vg_o[rs, :] = z[:, EV_V:EV_G].astype(BF16)
    g_o[rs, :] = z[:, EV_G:EV_CQ]
    cq = z[:, EV_CQ:EV_CKV]
    ckv = z[:, EV_CKV:EV_A]
    blk_a = z[:, EV_A:EV_B]
    blk_b = z[:, EV_B:EV_WIDTH]
    al = jnp.dot(blk_a.astype(BF16), walpha_ref[...], preferred_element_type=F32) + balpha_ref[...]
    cqn = _rms(cq, qn_ref[...]).astype(BF16)
    q2 = jnp.dot(cqn, wq_ref[...], preferred_element_type=F32)
    ckvn = _rms(ckv, kvn_ref[...]).astype(BF16)
    krr = (blk_a * cosk_ref[rs, :] + blk_b * sink_ref[rs, :]).astype(BF16)
    kk = (jnp.dot(ckvn, wk_ref[...], preferred_element_type=F32)
          + jnp.dot(krr, e_ref[...], preferred_element_type=F32))
    vv = jnp.dot(ckvn, wv_ref[...], preferred_element_type=F32)
    yield
    lg_o[rs, :] = (jnp.minimum(al, 0.0) - jnp.log1p(jnp.exp(-jnp.abs(al)))) * (1.0 / GATE_NORM)
    cosq = cosq_ref[rs, :]
    sinq = sinq_ref[rs, :]
    scale = (NOPE_B + ROPE_B) ** -0.5 * math.log2(math.e)
    for hh in range(H_B):
        lo = hh * HEAD_SLOT
        qs = q2[:, lo:lo + HEAD_SLOT] * cosq + q2[:, MLA_W + lo:MLA_W + lo + HEAD_SLOT] * sinq
        q_o[rs, lo:lo + HEAD_SLOT] = (qs * scale).astype(BF16)
    k_o[rs, :] = kk.astype(BF16)
    vlane = lax.broadcasted_iota(jnp.int32, (1, MLA_W), 1) & (HEAD_SLOT - 1)
    ones_lane = jnp.where(vlane == V_B, 1.0, 0.0)
    vt_o[:, rs] = (vv + ones_lane).T.astype(BF16)
    return x


def _even_out_stage(x, refs, outs, rs):
    mod_ref, gpost_ref, of_ref, ob_ref, g_ref, gain_ref, ao_ref, wa_ref, wb_ref = refs
    o = of_ref[rs, :] + ob_ref[rs, :]
    g = g_ref[rs, :]
    gain = gain_ref[...]
    parts = []
    for hh in range(H_A):
        sl = slice(hh * DV_A, (hh + 1) * DV_A)
        parts.append(_rms(o[:, sl], gain) * _silu(g[:, sl]))
    oa = jnp.concatenate(parts, axis=-1).astype(BF16)
    mix = (jnp.dot(oa, wa_ref[...], preferred_element_type=F32)
           + jnp.dot(ao_ref[rs, :], wb_ref[...], preferred_element_type=F32))
    return _residual(x, mix, mod_ref, gpost_ref[...], 1.0)


def _odd_in_stage(x, refs, outs, rs):
    mod_ref, gpre_ref, win_ref, sgun_ref, ws_ref, bs_ref = refs
    zc_o, od_o = outs
    h = _modulated(x, mod_ref, gpre_ref[...]).astype(BF16)
    z = jnp.dot(h, win_ref[...], preferred_element_type=F32)
    yield
    zc_o[rs, :] = z[:, :W_C].astype(BF16)
    zd = z[:, W_C:]
    ge = 0.5 * zd * (1.0 + lax.erf(zd * (2.0 ** -0.5)))
    u = ge[:, :W_D]
    vn = _rms(ge[:, W_D:], sgun_ref[...]).astype(BF16)
    tiles = [(slice(c * SGU_CHUNK, (c + 1) * SGU_CHUNK), hh, slice(hh * DG_D, (hh + 1) * DG_D))
             for c in range(x.shape[0] // SGU_CHUNK) for hh in range(H_D)]
    svs = [jnp.dot(ws_ref[hh], vn[cr, cs], preferred_element_type=F32) for cr, hh, cs in tiles]
    yield
    for (cr, hh, cs), sv in zip(tiles, svs):
        orow = slice(rs.start + cr.start, rs.start + cr.stop)
        od_o[orow, cs] = (u[cr, cs] * (sv + bs_ref[hh])).astype(BF16)
    return x


def _odd_out_stage(x, refs, outs, rs):
    mod_ref, gpost_ref, fc_ref, od_ref, wa_ref, wb_ref = refs
    mix = (jnp.dot(fc_ref[rs, :], wa_ref[...], preferred_element_type=F32)
           + jnp.dot(od_ref[rs, :], wb_ref[...], preferred_element_type=F32))
    return _residual(x, mix, mod_ref, gpost_ref[...], 1.0)


def _chain_kernel(*refs, fns, counts, rows, write_x):
    n_in = 1 + sum(c[0] for c in counts)
    x_ref = refs[0]
    stage_refs = []
    i, o = 1, n_in + int(write_x)
    for n_args, n_outs in counts:
        stage_refs.append((refs[i:i + n_args], refs[o:o + n_outs]))
        i += n_args
        o += n_outs
    groups = [slice(r * rows, (r + 1) * rows) for r in range(x_ref.shape[0] // rows)]
    xs = [x_ref[rs, :] for rs in groups]
    for fn, (ins, outs) in zip(fns, stage_refs):
        xs = _lockstep([fn(x, ins, outs, rs) for x, rs in zip(xs, groups)])
    if write_x:
        for x, rs in zip(xs, groups):
            refs[n_in][rs, :] = x


def _lockstep(results):
    out = list(results)
    live = [i for i, r in enumerate(out) if hasattr(r, "send")]
    while live:
        for i in list(live):
            try:
                next(out[i])
            except StopIteration as done:
                out[i] = done.value
                live.remove(i)
    return out


def _chain(x, stages, tm, name, write_x=True):
    B, S, D = x.shape
    tm = min(tm, S)
    arrays, specs = [x], [_tok_spec(tm, D)]
    out_shapes = [jax.ShapeDtypeStruct((B, S, D), F32)] if write_x else []
    out_specs = [_tok_spec(tm, D)] if write_x else []
    counts = []
    for _, ins, outs in stages:
        arrays += [a for a, _ in ins]
        specs += [s(tm) for _, s in ins]
        for w, dt, *transposed in outs:
            if transposed:
                out_shapes.append(jax.ShapeDtypeStruct((B, w, S), dt))
                out_specs.append(pl.BlockSpec((None, w, tm), lambda b, i: (b, 0, i)))
            else:
                out_shapes.append(jax.ShapeDtypeStruct((B, S, w), dt))
                out_specs.append(_tok_spec(tm, w))
        counts.append((len(ins), len(outs)))
    return pl.pallas_call(
        functools.partial(_chain_kernel, fns=tuple(s[0] for s in stages), counts=tuple(counts),
                          rows=min(CHAIN_ROWS, tm), write_x=write_x),
        grid=(B, S // tm),
        in_specs=specs,
        out_specs=out_specs,
        out_shape=out_shapes,
        compiler_params=_cparams(("arbitrary", "arbitrary")),
        name=name,
    )(*arrays)


def _res(a):
    return (a, lambda tm: _resident(a.shape))


def _tok(a):
    return (a, lambda tm: _tok_spec(tm, a.shape[-1]))


def _pos(a):
    return (a, lambda tm: pl.BlockSpec((tm, a.shape[-1]), lambda b, i: (i, 0)))


def _ffn_stage_spec(mod5, layer, sub, which, b_off, g_pre, g_post, w13, w2):
    wspec = lambda r, c: (lambda tm: pl.BlockSpec((None, None, r, c), lambda b, i: (layer, which, 0, 0),
                                                  pipeline_mode=pl.Buffered(1)))
    ins = [(mod5, lambda tm: _mod_spec(layer, sub, b_off)), _res(g_pre), _res(g_post),
           (w13, wspec(D_MODEL, 2 * D_FF)), (w2, wspec(D_FF, D_MODEL))]
    return (_ffn_stage, ins, [])


def _even_in_stage_spec(mod5, layer, b_off, g_pre, wp, tabs):
    ins = [(mod5, lambda tm: _mod_spec(layer, 1, b_off)), _res(g_pre), _res(wp["w_in"]), _res(wp["w_alpha"]),
           _res(wp["b_alpha"]), _res(wp["q_norm"]), _res(wp["w_q"]), _res(wp["kv_norm"]), _res(wp["w_k"]),
           _res(wp["w_v"]), _res(wp["e_rope"]),
           _pos(tabs["cosq"]), _pos(tabs["sinq"]), _pos(tabs["cosk"]), _pos(tabs["sink"])]
    outs = [(256, BF16), (256, BF16), (512, BF16), (512, F32), (512, F32), (MLA_W, BF16), (MLA_W, BF16),
            (MLA_W, BF16, True)]
    return (_even_in_stage, ins, outs)


def _even_out_stage_spec(mod5, layer, b_off, g_post, o_f, o_b, g, ao, wp):
    ins = [(mod5, lambda tm: _mod_spec(layer, 1, b_off)), _res(g_post), _tok(o_f), _tok(o_b), _tok(g),
           _res(wp["gain"]), _tok(ao), _res(wp["wo_a"]), _res(wp["wo_b"])]
    return (_even_out_stage, ins, [])


def _odd_in_stage_spec(mod5, layer, b_off, g_pre, w_in, sgu_norm, w_s, b_s):
    ins = [(mod5, lambda tm: _mod_spec(layer, 1, b_off)), _res(g_pre), _res(w_in), _res(sgu_norm), _res(w_s), _res(b_s)]
    return (_odd_in_stage, ins, [(W_C, BF16), (W_D, BF16)])


def _odd_out_stage_spec(mod5, layer, b_off, g_post, fc, od, wa, wb):
    ins = [(mod5, lambda tm: _mod_spec(layer, 1, b_off)), _res(g_post), _tok(fc), _tok(od), _res(wa), _res(wb)]
    return (_odd_out_stage, ins, [])


def _gla_kernel(qf_ref, kf_ref, lgf_ref, vf_ref, qb_ref, kb_ref, lgb_ref, vb_ref,
                cumf_ref, cumb_ref, of_ref, ob_ref, sf_ref, sb_ref):
    @pl.when(pl.program_id(2) == 0)
    def _():
        sf_ref[...] = jnp.zeros_like(sf_ref)
        sb_ref[...] = jnp.zeros_like(sb_ref)

    ts = GLA_TS
    nchunk = ts // GLA_CHUNK
    nblk = qf_ref.shape[0] // ts
    r = lax.broadcasted_iota(jnp.int32, (ts, ts), 0)
    c = lax.broadcasted_iota(jnp.int32, (ts, ts), 1)
    same = (r >> 6) == (c >> 6)
    lane = lax.broadcasted_iota(jnp.int32, (ts, 2 * DK_A), 1)
    row = lax.broadcasted_iota(jnp.int32, (ts, 2 * DV_A), 0)
    srow = lax.broadcasted_iota(jnp.int32, (2 * DK_A, 2 * DV_A), 0)
    scol = lax.broadcasted_iota(jnp.int32, (2 * DK_A, 2 * DV_A), 1)
    own = (srow >= DK_A) == (scol >= DV_A)
    scale = DK_A ** -0.5

    class Inst:
        pass

    insts = []
    for i in range(nblk):
        for rev in (False, True):
            it = Inst()
            it.rev = rev
            blk = (nblk - 1 - i) if rev else i
            it.rows = slice(blk * ts, (blk + 1) * ts)
            it.q_ref, it.k_ref, it.lg_ref, it.v_ref = (qb_ref, kb_ref, lgb_ref, vb_ref) if rev else \
                                                      (qf_ref, kf_ref, lgf_ref, vf_ref)
            it.cum_ref = cumb_ref if rev else cumf_ref
            it.amask = same & ((c > r) if rev else (c <= r))
            it.edge = 0 if rev else GLA_CHUNK - 1
            it.o_ref = ob_ref if rev else of_ref
            insts.append(it)

    for it in insts:
        it.lg = it.lg_ref[it.rows, :]
        hi = it.lg.astype(BF16)
        lo = (it.lg - hi.astype(F32)).astype(BF16)
        cum = it.cum_ref[...]
        it.b = jnp.dot(cum, hi, preferred_element_type=F32) + jnp.dot(cum, lo, preferred_element_type=F32)
    for it in insts:
        q = it.q_ref[it.rows, :].astype(F32) * scale
        k = it.k_ref[it.rows, :].astype(F32)
        b = it.b
        tot = jnp.concatenate(
            [jnp.broadcast_to(b[n * GLA_CHUNK + it.edge:n * GLA_CHUNK + it.edge + 1, :], (GLA_CHUNK, 2 * DK_A))
             for n in range(nchunk)], axis=0)
        qt = q * jnp.exp((b - it.lg) if it.rev else b)
        kt = (k * jnp.exp(-b)).astype(BF16)
        it.kdt = (k * jnp.exp(tot - b)).T.astype(BF16)
        it.bt = b.T
        it.qtb = qt.astype(BF16)
        it.a = []
        for hh in range(2):
            qh = jnp.where((lane >= hh * DK_A) & (lane < (hh + 1) * DK_A), qt, 0.0).astype(BF16)
            it.a.append(lax.dot_general(qh, kt, (((1,), (1,)), ((), ())), preferred_element_type=F32))
    for it in insts:
        v = it.v_ref[it.rows, :]
        it.intra = jnp.concatenate(
            [jnp.dot(jnp.where(it.amask, it.a[hh], 0.0).astype(BF16), v[:, hh * DV_A:(hh + 1) * DV_A],
                     preferred_element_type=F32) for hh in range(2)], axis=-1)
        it.kv = []
        for n in range(nchunk):
            r0 = n * GLA_CHUNK
            vn = jnp.where((row >= r0) & (row < r0 + GLA_CHUNK), v, jnp.zeros_like(v))
            it.kv.append(jnp.dot(it.kdt, vn, preferred_element_type=F32))
    state = {False: sf_ref[...], True: sb_ref[...]}
    for it in insts:
        it.inter = [None] * nchunk
    for i in range(nblk):
        pair = insts[2 * i:2 * i + 2]
        for step in range(nchunk):
            for it in pair:
                n = (nchunk - 1 - step) if it.rev else step
                r0 = n * GLA_CHUNK
                st = state[it.rev]
                it.inter[n] = jnp.dot(it.qtb[r0:r0 + GLA_CHUNK, :], st.astype(BF16), preferred_element_type=F32)
                dec = jnp.exp(it.bt[:, r0 + it.edge:r0 + it.edge + 1])
                state[it.rev] = jnp.where(own, dec * st + it.kv[n], 0.0)
        for it in pair:
            it.o_ref[it.rows, :] = it.intra + jnp.concatenate(it.inter, axis=0)
    sf_ref[...] = state[False]
    sb_ref[...] = state[True]


def _gla(qg, kg, vg, lg, consts):
    B, S, _ = qg.shape
    ts = min(GLA_STEP, S)
    nt = S // ts
    fwd = lambda w, off: pl.BlockSpec((None, ts, w), lambda b, p, t: (b, t, p + off))
    bwd = lambda w, off: pl.BlockSpec((None, ts, w), lambda b, p, t: (b, nt - 1 - t, p + off))
    npair = H_A // 2
    return pl.pallas_call(
        _gla_kernel,
        grid=(B, npair, nt),
        in_specs=[
            fwd(128, 0), fwd(128, 0), fwd(128, 0), fwd(256, 0),
            bwd(128, 0), bwd(128, 0), bwd(128, npair), bwd(256, 0),
            _resident((GLA_TS, GLA_TS)), _resident((GLA_TS, GLA_TS)),
        ],
        out_specs=[fwd(256, 0), bwd(256, 0)],
        out_shape=[jax.ShapeDtypeStruct((B, S, H_A * DV_A), F32)] * 2,
        scratch_shapes=[pltpu.VMEM((2 * DK_A, 2 * DV_A), F32)] * 2,
        compiler_params=_cparams(("arbitrary", "arbitrary", "arbitrary")),
        name="gla",
    )(qg, kg, lg, vg, qg, kg, lg, vg, consts["cum_f"], consts["cum_b"])


def _attn_kernel(q_ref, k_ref, vt_ref, o_ref, *, tk, nk):
    tq = q_ref.shape[0]
    slots = [slice(h * HEAD_SLOT, (h + 1) * HEAD_SLOT) for h in range(ATT_HEADS)]
    qts = [q_ref[:, sl].astype(F32).T.astype(BF16) for sl in slots]

    group = min(ATT_UNROLL, nk)

    def scores(h, start):
        return jnp.dot(k_ref[pl.ds(start, tk), slots[h]], qts[h], preferred_element_type=F32)

    def absorb(h, start, st, state):
        m, acc = state
        m_new = jnp.maximum(m, jnp.max(st, axis=0, keepdims=True))
        pt = jnp.exp2(st - m_new).astype(BF16)
        acc = jnp.exp2(m - m_new) * acc + jnp.dot(vt_ref[slots[h], pl.ds(start, tk)], pt,
                                                  preferred_element_type=F32)
        return m_new, acc

    def body(j, carry):
        items = [(pl.multiple_of((j * group + g) * tk, tk), h) for g in range(group) for h in range(ATT_HEADS)]
        state = list(carry)
        pending = [scores(h, start) for start, h in items[:ATT_AHEAD]]
        for i, (start, h) in enumerate(items):
            if i + ATT_AHEAD < len(items):
                nstart, nh = items[i + ATT_AHEAD]
                pending.append(scores(nh, nstart))
            state[h] = absorb(h, start, pending.pop(0), state[h])
        return tuple(state)

    init = tuple((jnp.full((1, tq), -jnp.inf, F32), jnp.zeros((HEAD_SLOT, tq), F32)) for _ in slots)
    res = lax.fori_loop(0, nk // group, body, init)
    for (_, acc), sl in zip(res, slots):
        o_ref[:, sl] = (acc / acc[V_B:V_B + 1, :]).T.astype(BF16)


def _attn(q, k, vt):
    B, S, _ = q.shape
    tq = min(ATT_TQ, S)
    tk = min(ATT_TK, S)
    w = ATT_HEADS * HEAD_SLOT
    return pl.pallas_call(
        functools.partial(_attn_kernel, tk=tk, nk=S // tk),
        grid=(B, H_B // ATT_HEADS, S // tq),
        in_specs=[
            pl.BlockSpec((None, tq, w), lambda b, h, i: (b, i, h)),
            pl.BlockSpec((None, S, w), lambda b, h, i: (b, 0, h), pipeline_mode=pl.Buffered(1)),
            pl.BlockSpec((None, w, S), lambda b, h, i: (b, h, 0), pipeline_mode=pl.Buffered(1)),
        ],
        out_specs=pl.BlockSpec((None, tq, w), lambda b, h, i: (b, i, h)),
        out_shape=jax.ShapeDtypeStruct((B, S, MLA_W), BF16),
        compiler_params=_cparams(("arbitrary", "arbitrary", "arbitrary")),
        name="mla_attn",
    )(q, k, vt)


def _fnet1_kernel(f_ref, z_ref, a_ref):
    a_ref[...] = jnp.dot(f_ref[...], z_ref[...], preferred_element_type=F32).astype(BF16)


def _fnet2_kernel(a_ref, cs_ref, g_ref, o_ref, *, kt, n2, norm):
    prods = []
    for j in range(kt):
        a2 = a_ref[:, j].reshape(2 * n2, W_C)
        prods.append([jnp.dot(a2[:, gg * CG_C:(gg + 1) * CG_C], cs_ref[...], preferred_element_type=F32)
                      for gg in range(H_C)])
    for j in range(kt):
        br = [p[:n2, :CG_C] + p[n2:, CG_C:] for p in prods[j]]
        bi = [p[n2:, :CG_C] - p[:n2, CG_C:] for p in prods[j]]
        b2 = jnp.concatenate([jnp.concatenate(br, axis=-1), jnp.concatenate(bi, axis=-1)], axis=0).astype(BF16)
        zz = jnp.dot(g_ref[j], b2, preferred_element_type=F32)
        o_ref[:, j * W_C:(j + 1) * W_C] = (zz * norm).astype(BF16)


def _fnet(zc, consts):
    B, S, wc = zc.shape
    n1, n2 = consts["n1"], consts["n2"]
    tc = min(n2 * wc, 8192)
    a = pl.pallas_call(
        _fnet1_kernel,
        grid=(B, n2 * wc // tc),
        in_specs=[_resident((2 * n1, n1)), pl.BlockSpec((None, n1, tc), lambda b, j: (b, 0, j))],
        out_specs=pl.BlockSpec((None, 2 * n1, tc), lambda b, j: (b, 0, j)),
        out_shape=jax.ShapeDtypeStruct((B, 2 * n1, n2 * wc), BF16),
        compiler_params=_cparams(("arbitrary", "arbitrary")),
        name="fnet_stage1",
    )(consts["f1"], zc.reshape(B, n1, n2 * wc))
    kt = 8
    out = pl.pallas_call(
        functools.partial(_fnet2_kernel, kt=kt, n2=n2, norm=float(1.0 / math.sqrt(S * CG_C))),
        grid=(B, n1 // kt),
        in_specs=[
            pl.BlockSpec((None, 2, kt, n2, wc), lambda b, j: (b, 0, j, 0, 0)),
            _resident((CG_C, 2 * CG_C)),
            pl.BlockSpec((kt, n2, 2 * n2), lambda b, j: (j, 0, 0)),
        ],
        out_specs=pl.BlockSpec((None, n2, kt * wc), lambda b, j: (b, 0, j)),
        out_shape=jax.ShapeDtypeStruct((B, n2, n1 * wc), BF16),
        compiler_params=_cparams(("arbitrary", "arbitrary")),
        name="fnet_stage2",
    )(a.reshape(B, 2, n1, n2, wc), consts["cs"], consts["g"])
    return out.reshape(B, S, wc)


def _gla_consts():
    ts = GLA_TS
    r = np.arange(ts)[:, None]
    c = np.arange(ts)[None, :]
    same = (r // GLA_CHUNK) == (c // GLA_CHUNK)
    as_bf16 = lambda m: jnp.asarray(m.astype(np.float32), dtype=BF16)
    return {"cum_f": as_bf16(same & (c <= r)), "cum_b": as_bf16(same & (c >= r))}


def _rope_tables(S):
    half = ROPE_B // 2
    inv = ROPE_BASE ** (-jnp.arange(half, dtype=F32) / half)
    ang = jnp.arange(S, dtype=F32)[:, None] * inv[None, :]
    cos, sin = jnp.cos(ang), jnp.sin(ang)
    cc = jnp.concatenate([cos, cos], axis=-1)
    ss = jnp.concatenate([sin, sin], axis=-1)
    z = lambda w: jnp.zeros((S, w), F32)
    return {
        "cosq": jnp.concatenate([jnp.ones((S, NOPE_B), F32), cc, z(HEAD_SLOT - NOPE_B - ROPE_B)], axis=-1),
        "sinq": jnp.concatenate([z(NOPE_B), ss, z(HEAD_SLOT - NOPE_B - ROPE_B)], axis=-1),
        "cosk": jnp.concatenate([cc, z(LANES - ROPE_B)], axis=-1),
        "sink": jnp.concatenate([ss, z(LANES - ROPE_B)], axis=-1),
    }


def _fnet_consts(S):
    n1 = 1 << (int(math.log2(S)) // 2)
    n2 = S // n1
    two_pi = 2.0 * math.pi
    k1 = jnp.arange(n1, dtype=jnp.int32)
    ph1 = ((k1[:, None] * k1[None, :]) % n1).astype(F32) * (two_pi / n1)
    f1 = jnp.concatenate([jnp.cos(ph1), -jnp.sin(ph1)], axis=0).astype(BF16)
    ch = jnp.arange(CG_C, dtype=jnp.int32)
    phc = ((ch[:, None] * ch[None, :]) % CG_C).astype(F32) * (two_pi / CG_C)
    cs = jnp.concatenate([jnp.cos(phc), jnp.sin(phc)], axis=-1).astype(BF16)
    s2 = jnp.arange(n2, dtype=jnp.int32)
    k = k1[:, None, None] + n1 * s2[None, :, None]
    ph = ((k * s2[None, None, :]) % S).astype(F32) * (two_pi / S)
    g = jnp.concatenate([jnp.cos(ph), jnp.sin(ph)], axis=-1).astype(BF16)
    return {"n1": n1, "n2": n2, "f1": f1, "cs": cs, "g": g}


def _even_weights(w_in, w_out, w_alpha, b_alpha, gla_gain, q_norm, w_q_b, kv_norm, w_kv_b):
    D = D_MODEL
    offs = np.cumsum([0, 256, 256, 512, 512, 2 * ALPHA_RANK, Q_RANK, KV_RANK, ROPE_B])
    col = lambda i: w_in[:, offs[i]:offs[i + 1]]
    kr = col(7)
    half = ROPE_B // 2
    kr_rot = jnp.concatenate([-kr[:, half:], kr[:, :half]], axis=-1)
    zc = lambda w: jnp.zeros((D, w), F32)
    win = jnp.concatenate([col(0), col(1), col(2), col(3), col(5), col(6),
                           kr, col(4), zc(64), kr_rot, zc(96)], axis=-1).astype(BF16)
    assert win.shape[1] == EV_WIDTH
    wal = jnp.zeros((LANES, 512), F32)
    wal = wal.at[ROPE_B:ROPE_B + ALPHA_RANK, :256].set(w_alpha[0])
    wal = wal.at[ROPE_B + ALPHA_RANK:ROPE_B + 2 * ALPHA_RANK, 256:].set(w_alpha[1])
    bal = jnp.concatenate([b_alpha[0], b_alpha[1]])[None, :]
    wq = w_q_b.reshape(Q_RANK, H_B, NOPE_B + ROPE_B)
    qn, qr = wq[..., :NOPE_B], wq[..., NOPE_B:]
    pad = HEAD_SLOT - NOPE_B - ROPE_B
    zq = lambda w: jnp.zeros((Q_RANK, H_B, w), F32)
    wq_main = jnp.concatenate([qn, qr, zq(pad)], axis=-1).reshape(Q_RANK, MLA_W)
    wq_rot = jnp.concatenate([zq(NOPE_B), -qr[..., half:], qr[..., :half], zq(pad)], axis=-1).reshape(Q_RANK, MLA_W)
    wq2 = jnp.concatenate([wq_main, wq_rot], axis=-1).astype(BF16)
    wkv = w_kv_b.reshape(KV_RANK, H_B, NOPE_B + V_B)
    zk = lambda w: jnp.zeros((KV_RANK, H_B, w), F32)
    wk = jnp.concatenate([wkv[..., :NOPE_B], zk(HEAD_SLOT - NOPE_B)], axis=-1).reshape(KV_RANK, MLA_W).astype(BF16)
    wv = jnp.concatenate([wkv[..., NOPE_B:], zk(HEAD_SLOT - V_B)], axis=-1).reshape(KV_RANK, MLA_W).astype(BF16)
    e = np.zeros((LANES, H_B, HEAD_SLOT), np.float32)
    for r in range(ROPE_B):
        e[r, :, NOPE_B + r] = 1.0
    e = jnp.asarray(e.reshape(LANES, MLA_W), dtype=BF16)
    wo_a = w_out[:H_A * DV_A].astype(BF16)
    wo_b = jnp.concatenate([w_out[H_A * DV_A:].reshape(H_B, V_B, D),
                            jnp.zeros((H_B, HEAD_SLOT - V_B, D), F32)], axis=1).reshape(MLA_W, D).astype(BF16)
    return {"w_in": win, "w_alpha": wal.astype(BF16), "b_alpha": bal, "q_norm": q_norm[None, :], "w_q": wq2,
            "kv_norm": kv_norm[None, :], "w_k": wk, "w_v": wv, "e_rope": e, "gain": gla_gain[None, :],
            "wo_a": wo_a, "wo_b": wo_b}


def kernel(x_prompt, x_sample, c_prompt, c_sample, ada_w, ada_b, norm_pre, norm_post, ffn_w13, ffn_w2,
           ev_w_in, ev_w_out, gla_w_alpha, gla_b_alpha, gla_norm, mla_q_norm, mla_w_q_b, mla_kv_norm,
           mla_w_kv_b, od_w_in, od_w_out, sgu_norm, sgu_w_s, sgu_b):
    groups = [(x_prompt, 0), (x_sample, x_prompt.shape[0])]
    c_all = jnp.concatenate([c_prompt, c_sample], axis=0)
    mod5 = _ada(c_all, ada_w, ada_b).reshape(DEPTH, c_all.shape[0], 3, 3, D_MODEL)

    w13 = ffn_w13.astype(BF16)
    w2 = ffn_w2.astype(BF16)
    gla_consts = _gla_consts()
    rope = {x.shape[1]: _rope_tables(x.shape[1]) for x, _ in groups}
    fnet = {x.shape[1]: _fnet_consts(x.shape[1]) for x, _ in groups}
    even_w = [_even_weights(ev_w_in[i], ev_w_out[i], gla_w_alpha[i], gla_b_alpha[i], gla_norm[i], mla_q_norm[i],
                            mla_w_q_b[i], mla_kv_norm[i], mla_w_kv_b[i]) for i in range(ev_w_in.shape[0])]
    od_win = od_w_in.astype(BF16)
    od_wout = od_w_out.astype(BF16)
    sgu_ws = sgu_w_s.astype(BF16)
    sgu_bias = jnp.broadcast_to(sgu_b[..., None], sgu_b.shape + (DG_D,))

    outs = []
    for x, b_off in groups:
        S = x.shape[1]
        for l in range(DEPTH):
            pre = lambda s: norm_pre[l, s][None, :]
            post = lambda s: norm_post[l, s][None, :]
            ffn0 = _ffn_stage_spec(mod5, l, 0, 0, b_off, pre(0), post(0), w13, w2)
            ffn2 = _ffn_stage_spec(mod5, l, 2, 1, b_off, pre(2), post(2), w13, w2)
            i = l // 2
            if l % 2 == 0:
                wp = even_w[i]
                (x,) = _chain(x, [ffn0], CHAIN_TM, "ffn")
                qg, kg, vg, g, lg, q, k, vt = _chain(
                    x, [_even_in_stage_spec(mod5, l, b_off, pre(1), wp, rope[S])], CHAIN_TM, "even_in", write_x=False)
                o_f, o_b = _gla(qg, kg, vg, lg, gla_consts)
                ao = _attn(q, k, vt)
                (x,) = _chain(x, [_even_out_stage_spec(mod5, l, b_off, post(1), o_f, o_b, g, ao, wp), ffn2],
                              CHAIN_TM, "even_out_ffn")
            else:
                (x,) = _chain(x, [ffn0], CHAIN_TM, "ffn")
                zc, od = _chain(
                    x, [_odd_in_stage_spec(mod5, l, b_off, pre(1), od_win[i], sgu_norm[i][None, :], sgu_ws[i],
                                           sgu_bias[i])], CHAIN_TM, "odd_in", write_x=False)
                fc = _fnet(zc, fnet[S])
                (x,) = _chain(x, [_odd_out_stage_spec(mod5, l, b_off, post(1), fc, od, od_wout[i, :W_C],
                                                      od_wout[i, W_C:]), ffn2], CHAIN_TM, "odd_out_ffn")
        outs.append(x)
    return tuple(outs)
```

```python
import functools
import math

import jax
import jax.numpy as jnp
import numpy as np
from jax import lax
from jax.experimental import pallas as pl
from jax.experimental.pallas import tpu as pltpu

F32 = jnp.float32
BF16 = jnp.bfloat16

D_MODEL = 1024
DEPTH = 4
D_FF = 2816
EPS = 1e-6
H_A, DK_A, DV_A = 4, 64, 128
ALPHA_RANK = 16
GATE_NORM = 16.0
GLA_CHUNK = 64
H_B, Q_RANK, KV_RANK, NOPE_B, ROPE_B, V_B = 8, 256, 128, 64, 32, 64
ROPE_BASE = 10000.0
H_C, CG_C = 4, 128
H_D, DG_D = 4, 128
SGU_CHUNK = 128
HEAD_SLOT = 128
MLA_W = H_B * HEAD_SLOT
W_C = H_C * CG_C
W_D = H_D * DG_D

VMEM_LIMIT_BYTES = 56 * 1024 * 1024
LANES = 128

CHAIN_TM = 512
FFN_TM = 1024
CHAIN_TM_EVEN_IN = 256
CHAIN_ROWS = 256
FFN_CHUNKS = (1024, 1024, 768)
GLA_TS = 256
GLA_STEP = 2048
ATT_TQ = 1024
ATT_TK = 2048
ATT_HEADS = 2
ATT_UNROLL = 4
ATT_AHEAD = 1


def _cparams(sem):
    return pltpu.CompilerParams(dimension_semantics=sem, vmem_limit_bytes=VMEM_LIMIT_BYTES)


def _resident(shape):
    nd = len(shape)
    return pl.BlockSpec(shape, lambda *_: (0,) * nd, pipeline_mode=pl.Buffered(1))


def _rms(x, g):
    return x * lax.rsqrt(jnp.mean(x * x, axis=-1, keepdims=True) + EPS) * g


def _modulated(x, mod_ref, g_pre):
    shift = mod_ref[0:1, :]
    scale = mod_ref[1:2, :]
    return _rms(x, g_pre) * (1.0 + scale) + shift


def _residual(x, y, mod_ref, g_post, w_res):
    gate = mod_ref[2:3, :]
    return x + (w_res * (1.0 + gate)) * _rms(y, g_post)


def _silu(a):
    return a * jax.nn.sigmoid(a)


def _mod_spec(layer, sub, b_off):
    return pl.BlockSpec((None, None, None, 3, D_MODEL), lambda b, i: (layer, b_off + b, sub, 0, 0))


def _tok_spec(tm, width):
    return pl.BlockSpec((None, tm, width), lambda b, i: (b, i, 0))


def _ada_kernel(c_ref, w_ref, b_ref, o_ref):
    cc = _silu(c_ref[...]).astype(BF16)
    o_ref[...] = jnp.dot(cc, w_ref[...].astype(BF16), preferred_element_type=F32) + b_ref[...]


def _ada(c_all, ada_w, ada_b):
    bt = c_all.shape[0]
    n = ada_w.shape[-1]
    tn = n // 8
    return pl.pallas_call(
        _ada_kernel,
        grid=(DEPTH, n // tn),
        in_specs=[
            pl.BlockSpec((bt, D_MODEL), lambda l, j: (0, 0)),
            pl.BlockSpec((None, D_MODEL, tn), lambda l, j: (l, 0, j)),
            pl.BlockSpec((None, 1, tn), lambda l, j: (l, 0, j)),
        ],
        out_specs=pl.BlockSpec((None, bt, tn), lambda l, j: (l, 0, j)),
        out_shape=jax.ShapeDtypeStruct((DEPTH, bt, n), F32),
        compiler_params=_cparams(("arbitrary", "arbitrary")),
        name="ada_mod",
    )(c_all, ada_w, ada_b.reshape(DEPTH, 1, n))


def _ffn_stage(x, refs, outs, rs):
    mod_ref, gpre_ref, gpost_ref, w13_ref, w2_ref = refs
    h = _modulated(x, mod_ref, gpre_ref[...]).astype(BF16)
    acc = None
    off = 0
    for ck in FFN_CHUNKS:
        a = jnp.dot(h, w13_ref[:, off:off + ck], preferred_element_type=F32)
        b = jnp.dot(h, w13_ref[:, D_FF + off:D_FF + off + ck], preferred_element_type=F32)
        act = (_silu(a) * b).astype(BF16)
        part = jnp.dot(act, w2_ref[off:off + ck, :], preferred_element_type=F32)
        acc = part if acc is None else acc + part
        off += ck
    return _residual(x, acc, mod_ref, gpost_ref[...], 0.5)


EV_Q, EV_K, EV_V, EV_G, EV_CQ, EV_CKV, EV_A, EV_B = 0, 256, 512, 1024, 1536, 1792, 1920, 2048
EV_WIDTH = 2176


def _even_in_stage(x, refs, outs, rs):
    (mod_ref, gpre_ref, win_ref, walpha_ref, balpha_ref, qn_ref, wq_ref, kvn_ref, wk_ref, wv_ref, e_ref,
     cosq_ref, sinq_ref, cosk_ref, sink_ref) = refs
    qg_o, kg_o, vg_o, g_o, lg_o, q_o, k_o, vt_o = outs
    h = _modulated(x, mod_ref, gpre_ref[...]).astype(BF16)
    z = jnp.dot(h, win_ref[...], preferred_element_type=F32)
    yield
    qg_o[rs, :] = z[:, EV_Q:EV_K].astype(BF16)
    kg_o[rs, :] = z[:, EV_K:EV_V].astype(BF16)
    vg_o[rs, :] = z[:, EV_V:EV_G].astype(BF16)
    g_o[rs, :] = z[:, EV_G:EV_CQ]
    cq = z[:, EV_CQ:EV_CKV]
    ckv = z[:, EV_CKV:EV_A]
    blk_a = z[:, EV_A:EV_B]
    blk_b = z[:, EV_B:EV_WIDTH]
    al = jnp.dot(blk_a.astype(BF16), walpha_ref[...], preferred_element_type=F32) + balpha_ref[...]
    cqn = _rms(cq, qn_ref[...]).astype(BF16)
    q2 = jnp.dot(cqn, wq_ref[...], preferred_element_type=F32)
    ckvn = _rms(ckv, kvn_ref[...]).astype(BF16)
    krr = (blk_a * cosk_ref[rs, :] + blk_b * sink_ref[rs, :]).astype(BF16)
    kk = (jnp.dot(ckvn, wk_ref[...], preferred_element_type=F32)
          + jnp.dot(krr, e_ref[...], preferred_element_type=F32))
    vv = jnp.dot(ckvn, wv_ref[...], preferred_element_type=F32)
    yield
    lg_o[rs, :] = (jnp.minimum(al, 0.0) - jnp.log1p(jnp.exp(-jnp.abs(al)))) * (1.0 / GATE_NORM)
    cosq = cosq_ref[rs, :]
    sinq = sinq_ref[rs, :]
    scale = (NOPE_B + ROPE_B) ** -0.5 * math.log2(math.e)
    for hh in range(H_B):
        lo = hh * HEAD_SLOT
        qs = q2[:, lo:lo + HEAD_SLOT] * cosq + q2[:, MLA_W + lo:MLA_W + lo + HEAD_SLOT] * sinq
        q_o[rs, lo:lo + HEAD_SLOT] = (qs * scale).astype(BF16)
    k_o[rs, :] = kk.astype(BF16)
    vlane = lax.broadcasted_iota(jnp.int32, (1, MLA_W), 1) & (HEAD_SLOT - 1)
    ones_lane = jnp.where(vlane == V_B, 1.0, 0.0)
    vt_o[:, rs] = (vv + ones_lane).T.astype(BF16)
    return x


def _even_out_stage(x, refs, outs, rs):
    mod_ref, gpost_ref, of_ref, ob_ref, g_ref, gain_ref, ao_ref, wa_ref, wb_ref = refs
    o = of_ref[rs, :] + ob_ref[rs, :]
    g = g_ref[rs, :]
    gain = gain_ref[...]
    parts = []
    for hh in range(H_A):
        sl = slice(hh * DV_A, (hh + 1) * DV_A)
        parts.append(_rms(o[:, sl], gain) * _silu(g[:, sl]))
    oa = jnp.concatenate(parts, axis=-1).astype(BF16)
    mix = (jnp.dot(oa, wa_ref[...], preferred_element_type=F32)
           + jnp.dot(ao_ref[rs, :], wb_ref[...], preferred_element_type=F32))
    return _residual(x, mix, mod_ref, gpost_ref[...], 1.0)


def _odd_in_stage(x, refs, outs, rs):
    mod_ref, gpre_ref, win_ref, sgun_ref, ws_ref, bs_ref = refs
    zc_o, od_o = outs
    h = _modulated(x, mod_ref, gpre_ref[...]).astype(BF16)
    z = jnp.dot(h, win_ref[...], preferred_element_type=F32)
    yield
    zc_o[rs, :] = z[:, :W_C].astype(BF16)
    zd = z[:, W_C:]
    ge = 0.5 * zd * (1.0 + lax.erf(zd * (2.0 ** -0.5)))
    u = ge[:, :W_D]
    vn = _rms(ge[:, W_D:], sgun_ref[...]).astype(BF16)
    tiles = [(slice(c * SGU_CHUNK, (c + 1) * SGU_CHUNK), hh, slice(hh * DG_D, (hh + 1) * DG_D))
             for c in range(x.shape[0] // SGU_CHUNK) for hh in range(H_D)]
    svs = [jnp.dot(ws_ref[hh], vn[cr, cs], preferred_element_type=F32) for cr, hh, cs in tiles]
    yield
    for (cr, hh, cs), sv in zip(tiles, svs):
        orow = slice(rs.start + cr.start, rs.start + cr.stop)
        od_o[orow, cs] = (u[cr, cs] * (sv + bs_ref[hh])).astype(BF16)
    return x


def _odd_out_stage(x, refs, outs, rs):
    mod_ref, gpost_ref, fc_ref, od_ref, wa_ref, wb_ref = refs
    mix = (jnp.dot(fc_ref[rs, :], wa_ref[...], preferred_element_type=F32)
           + jnp.dot(od_ref[rs, :], wb_ref[...], preferred_element_type=F32))
    return _residual(x, mix, mod_ref, gpost_ref[...], 1.0)


def _chain_kernel(*refs, fns, counts, rows, write_x):
    n_in = 1 + sum(c[0] for c in counts)
    x_ref = refs[0]
    stage_refs = []
    i, o = 1, n_in + int(write_x)
    for n_args, n_outs in counts:
        stage_refs.append((refs[i:i + n_args], refs[o:o + n_outs]))
        i += n_args
        o += n_outs
    groups = [slice(r * rows, (r + 1) * rows) for r in range(x_ref.shape[0] // rows)]
    xs = [x_ref[rs, :] for rs in groups]
    for fn, (ins, outs) in zip(fns, stage_refs):
        xs = _lockstep([fn(x, ins, outs, rs) for x, rs in zip(xs, groups)])
    if write_x:
        for x, rs in zip(xs, groups):
            refs[n_in][rs, :] = x


def _lockstep(results):
    out = list(results)
    live = [i for i, r in enumerate(out) if hasattr(r, "send")]
    while live:
        for i in list(live):
            try:
                next(out[i])
            except StopIteration as done:
                out[i] = done.value
                live.remove(i)
    return out


def _chain(x, stages, tm, name, write_x=True):
    B, S, D = x.shape
    tm = min(tm, S)
    arrays, specs = [x], [_tok_spec(tm, D)]
    out_shapes = [jax.ShapeDtypeStruct((B, S, D), F32)] if write_x else []
    out_specs = [_tok_spec(tm, D)] if write_x else []
    counts = []
    for _, ins, outs in stages:
        arrays += [a for a, _ in ins]
        specs += [s(tm) for _, s in ins]
        for w, dt, *transposed in outs:
            if transposed:
                out_shapes.append(jax.ShapeDtypeStruct((B, w, S), dt))
                out_specs.append(pl.BlockSpec((None, w, tm), lambda b, i: (b, 0, i)))
            else:
                out_shapes.append(jax.ShapeDtypeStruct((B, S, w), dt))
                out_specs.append(_tok_spec(tm, w))
        counts.append((len(ins), len(outs)))
    return pl.pallas_call(
        functools.partial(_chain_kernel, fns=tuple(s[0] for s in stages), counts=tuple(counts),
                          rows=min(CHAIN_ROWS, tm), write_x=write_x),
        grid=(B, S // tm),
        in_specs=specs,
        out_specs=out_specs,
        out_shape=out_shapes,
        compiler_params=_cparams(("arbitrary", "arbitrary")),
        name=name,
    )(*arrays)


def _res(a):
    return (a, lambda tm: _resident(a.shape))


def _tok(a):
    return (a, lambda tm: _tok_spec(tm, a.shape[-1]))


def _pos(a):
    return (a, lambda tm: pl.BlockSpec((tm, a.shape[-1]), lambda b, i: (i, 0)))


def _ffn_stage_spec(mod5, layer, sub, which, b_off, g_pre, g_post, w13, w2):
    wspec = lambda r, c: (lambda tm: pl.BlockSpec((None, None, r, c), lambda b, i: (layer, which, 0, 0),
                                                  pipeline_mode=pl.Buffered(1)))
    ins = [(mod5, lambda tm: _mod_spec(layer, sub, b_off)), _res(g_pre), _res(g_post),
           (w13, wspec(D_MODEL, 2 * D_FF)), (w2, wspec(D_FF, D_MODEL))]
    return (_ffn_stage, ins, [])


def _even_in_stage_spec(mod5, layer, b_off, g_pre, wp, tabs):
    ins = [(mod5, lambda tm: _mod_spec(layer, 1, b_off)), _res(g_pre), _res(wp["w_in"]), _res(wp["w_alpha"]),
           _res(wp["b_alpha"]), _res(wp["q_norm"]), _res(wp["w_q"]), _res(wp["kv_norm"]), _res(wp["w_k"]),
           _res(wp["w_v"]), _res(wp["e_rope"]),
           _pos(tabs["cosq"]), _pos(tabs["sinq"]), _pos(tabs["cosk"]), _pos(tabs["sink"])]
    outs = [(256, BF16), (256, BF16), (512, BF16), (512, F32), (512, F32), (MLA_W, BF16), (MLA_W, BF16),
            (MLA_W, BF16, True)]
    return (_even_in_stage, ins, outs)


def _even_out_stage_spec(mod5, layer, b_off, g_post, o_f, o_b, g, ao, wp):
    ins = [(mod5, lambda tm: _mod_spec(layer, 1, b_off)), _res(g_post), _tok(o_f), _tok(o_b), _tok(g),
           _res(wp["gain"]), _tok(ao), _res(wp["wo_a"]), _res(wp["wo_b"])]
    return (_even_out_stage, ins, [])


def _odd_in_stage_spec(mod5, layer, b_off, g_pre, w_in, sgu_norm, w_s, b_s):
    ins = [(mod5, lambda tm: _mod_spec(layer, 1, b_off)), _res(g_pre), _res(w_in), _res(sgu_norm), _res(w_s), _res(b_s)]
    return (_odd_in_stage, ins, [(W_C, BF16), (W_D, BF16)])


def _odd_out_stage_spec(mod5, layer, b_off, g_post, fc, od, wa, wb):
    ins = [(mod5, lambda tm: _mod_spec(layer, 1, b_off)), _res(g_post), _tok(fc), _tok(od), _res(wa), _res(wb)]
    return (_odd_out_stage, ins, [])


def _gla_kernel(qf_ref, kf_ref, lgf_ref, vf_ref, qb_ref, kb_ref, lgb_ref, vb_ref,
                cumf_ref, cumb_ref, of_ref, ob_ref, sf_ref, sb_ref):
    @pl.when(pl.program_id(2) == 0)
    def _():
        sf_ref[...] = jnp.zeros_like(sf_ref)
        sb_ref[...] = jnp.zeros_like(sb_ref)

    ts = GLA_TS
    nchunk = ts // GLA_CHUNK
    nblk = qf_ref.shape[0] // ts
    r = lax.broadcasted_iota(jnp.int32, (ts, ts), 0)
    c = lax.broadcasted_iota(jnp.int32, (ts, ts), 1)
    same = (r >> 6) == (c >> 6)
    lane = lax.broadcasted_iota(jnp.int32, (ts, 2 * DK_A), 1)
    row = lax.broadcasted_iota(jnp.int32, (ts, 2 * DV_A), 0)
    srow = lax.broadcasted_iota(jnp.int32, (2 * DK_A, 2 * DV_A), 0)
    scol = lax.broadcasted_iota(jnp.int32, (2 * DK_A, 2 * DV_A), 1)
    own = (srow >= DK_A) == (scol >= DV_A)
    scale = DK_A ** -0.5

    class Inst:
        pass

    insts = []
    for i in range(nblk):
        for rev in (False, True):
            it = Inst()
            it.rev = rev
            blk = (nblk - 1 - i) if rev else i
            it.rows = slice(blk * ts, (blk + 1) * ts)
            it.q_ref, it.k_ref, it.lg_ref, it.v_ref = (qb_ref, kb_ref, lgb_ref, vb_ref) if rev else \
                                                      (qf_ref, kf_ref, lgf_ref, vf_ref)
            it.cum_ref = cumb_ref if rev else cumf_ref
            it.amask = same & ((c > r) if rev else (c <= r))
            it.edge = 0 if rev else GLA_CHUNK - 1
            it.o_ref = ob_ref if rev else of_ref
            insts.append(it)

    for it in insts:
        it.lg = it.lg_ref[it.rows, :]
        hi = it.lg.astype(BF16)
        lo = (it.lg - hi.astype(F32)).astype(BF16)
        cum = it.cum_ref[...]
        it.b = jnp.dot(cum, hi, preferred_element_type=F32) + jnp.dot(cum, lo, preferred_element_type=F32)
    for it in insts:
        q = it.q_ref[it.rows, :].astype(F32) * scale
        k = it.k_ref[it.rows, :].astype(F32)
        b = it.b
        tot = jnp.concatenate(
            [jnp.broadcast_to(b[n * GLA_CHUNK + it.edge:n * GLA_CHUNK + it.edge + 1, :], (GLA_CHUNK, 2 * DK_A))
             for n in range(nchunk)], axis=0)
        qt = q * jnp.exp((b - it.lg) if it.rev else b)
        kt = (k * jnp.exp(-b)).astype(BF16)
        it.kdt = (k * jnp.exp(tot - b)).T.astype(BF16)
        it.bt = b.T
        it.qtb = qt.astype(BF16)
        it.a = []
        for hh in range(2):
            qh = jnp.where((lane >= hh * DK_A) & (lane < (hh + 1) * DK_A), qt, 0.0).astype(BF16)
            it.a.append(lax.dot_general(qh, kt, (((1,), (1,)), ((), ())), preferred_element_type=F32))
    for it in insts:
        v = it.v_ref[it.rows, :]
        it.intra = jnp.concatenate(
            [jnp.dot(jnp.where(it.amask, it.a[hh], 0.0).astype(BF16), v[:, hh * DV_A:(hh + 1) * DV_A],
                     preferred_element_type=F32) for hh in range(2)], axis=-1)
        it.kv = []
        for n in range(nchunk):
            r0 = n * GLA_CHUNK
            vn = jnp.where((row >= r0) & (row < r0 + GLA_CHUNK), v, jnp.zeros_like(v))
            it.kv.append(jnp.dot(it.kdt, vn, preferred_element_type=F32))
    state = {False: sf_ref[...], True: sb_ref[...]}
    for it in insts:
        it.inter = [None] * nchunk
    for i in range(nblk):
        pair = insts[2 * i:2 * i + 2]
        for step in range(nchunk):
            for it in pair:
                n = (nchunk - 1 - step) if it.rev else step
                r0 = n * GLA_CHUNK
                st = state[it.rev]
                it.inter[n] = jnp.dot(it.qtb[r0:r0 + GLA_CHUNK, :], st.astype(BF16), preferred_element_type=F32)
                dec = jnp.exp(it.bt[:, r0 + it.edge:r0 + it.edge + 1])
                state[it.rev] = jnp.where(own, dec * st + it.kv[n], 0.0)
        for it in pair:
            it.o_ref[it.rows, :] = it.intra + jnp.concatenate(it.inter, axis=0)
    sf_ref[...] = state[False]
    sb_ref[...] = state[True]


def _gla(qg, kg, vg, lg, consts):
    B, S, _ = qg.shape
    ts = min(GLA_STEP, S)
    nt = S // ts
    fwd = lambda w, off: pl.BlockSpec((None, ts, w), lambda b, p, t: (b, t, p + off))
    bwd = lambda w, off: pl.BlockSpec((None, ts, w), lambda b, p, t: (b, nt - 1 - t, p + off))
    npair = H_A // 2
    return pl.pallas_call(
        _gla_kernel,
        grid=(B, npair, nt),
        in_specs=[
            fwd(128, 0), fwd(128, 0), fwd(128, 0), fwd(256, 0),
            bwd(128, 0), bwd(128, 0), bwd(128, npair), bwd(256, 0),
            _resident((GLA_TS, GLA_TS)), _resident((GLA_TS, GLA_TS)),
        ],
        out_specs=[fwd(256, 0), bwd(256, 0)],
        out_shape=[jax.ShapeDtypeStruct((B, S, H_A * DV_A), F32)] * 2,
        scratch_shapes=[pltpu.VMEM((2 * DK_A, 2 * DV_A), F32)] * 2,
        compiler_params=_cparams(("arbitrary", "arbitrary", "arbitrary")),
        name="gla",
    )(qg, kg, lg, vg, qg, kg, lg, vg, consts["cum_f"], consts["cum_b"])


def _attn_kernel(q_ref, k_ref, vt_ref, o_ref, *, tk, nk):
    tq = q_ref.shape[0]
    slots = [slice(h * HEAD_SLOT, (h + 1) * HEAD_SLOT) for h in range(ATT_HEADS)]
    qts = [q_ref[:, sl].astype(F32).T.astype(BF16) for sl in slots]

    group = min(ATT_UNROLL, nk)

    def scores(h, start):
        return jnp.dot(k_ref[pl.ds(start, tk), slots[h]], qts[h], preferred_element_type=F32)

    def absorb(h, start, st, state):
        m, acc = state
        m_new = jnp.maximum(m, jnp.max(st, axis=0, keepdims=True))
        pt = jnp.exp2(st - m_new).astype(BF16)
        acc = jnp.exp2(m - m_new) * acc + jnp.dot(vt_ref[slots[h], pl.ds(start, tk)], pt,
                                                  preferred_element_type=F32)
        return m_new, acc

    def body(j, carry):
        items = [(pl.multiple_of((j * group + g) * tk, tk), h) for g in range(group) for h in range(ATT_HEADS)]
        state = list(carry)
        pending = [scores(h, start) for start, h in items[:ATT_AHEAD]]
        for i, (start, h) in enumerate(items):
            if i + ATT_AHEAD < len(items):
                nstart, nh = items[i + ATT_AHEAD]
                pending.append(scores(nh, nstart))
            state[h] = absorb(h, start, pending.pop(0), state[h])
        return tuple(state)

    init = tuple((jnp.full((1, tq), -jnp.inf, F32), jnp.zeros((HEAD_SLOT, tq), F32)) for _ in slots)
    res = lax.fori_loop(0, nk // group, body, init)
    for (_, acc), sl in zip(res, slots):
        o_ref[:, sl] = (acc / acc[V_B:V_B + 1, :]).T.astype(BF16)


def _attn(q, k, vt):
    B, S, _ = q.shape
    tq = min(ATT_TQ, S)
    tk = min(ATT_TK, S)
    w = ATT_HEADS * HEAD_SLOT
    kv_bufs = pl.Buffered(2 if 4 * S * w * 2 <= VMEM_LIMIT_BYTES // 4 else 1)
    return pl.pallas_call(
        functools.partial(_attn_kernel, tk=tk, nk=S // tk),
        grid=(B, H_B // ATT_HEADS, S // tq),
        in_specs=[
            pl.BlockSpec((None, tq, w), lambda b, h, i: (b, i, h)),
            pl.BlockSpec((None, S, w), lambda b, h, i: (b, 0, h), pipeline_mode=kv_bufs),
            pl.BlockSpec((None, w, S), lambda b, h, i: (b, h, 0), pipeline_mode=kv_bufs),
        ],
        out_specs=pl.BlockSpec((None, tq, w), lambda b, h, i: (b, i, h)),
        out_shape=jax.ShapeDtypeStruct((B, S, MLA_W), BF16),
        compiler_params=_cparams(("arbitrary", "arbitrary", "arbitrary")),
        name="mla_attn",
    )(q, k, vt)


def _fnet1_kernel(f_ref, z_ref, a_ref):
    a_ref[...] = jnp.dot(f_ref[...], z_ref[...], preferred_element_type=F32).astype(BF16)


def _fnet2_kernel(a_ref, cs_ref, g_ref, o_ref, *, kt, n2, norm):
    prods = []
    for j in range(kt):
        a2 = a_ref[:, j].reshape(2 * n2, W_C)
        prods.append([jnp.dot(a2[:, gg * CG_C:(gg + 1) * CG_C], cs_ref[...], preferred_element_type=F32)
                      for gg in range(H_C)])
    for j in range(kt):
        br = [p[:n2, :CG_C] + p[n2:, CG_C:] for p in prods[j]]
        bi = [p[n2:, :CG_C] - p[:n2, CG_C:] for p in prods[j]]
        b2 = jnp.concatenate([jnp.concatenate(br, axis=-1), jnp.concatenate(bi, axis=-1)], axis=0).astype(BF16)
        zz = jnp.dot(g_ref[j], b2, preferred_element_type=F32)
        o_ref[:, j * W_C:(j + 1) * W_C] = (zz * norm).astype(BF16)


def _fnet(zc, consts):
    B, S, wc = zc.shape
    n1, n2 = consts["n1"], consts["n2"]
    tc = min(n2 * wc, 8192)
    a = pl.pallas_call(
        _fnet1_kernel,
        grid=(B, n2 * wc // tc),
        in_specs=[_resident((2 * n1, n1)), pl.BlockSpec((None, n1, tc), lambda b, j: (b, 0, j))],
        out_specs=pl.BlockSpec((None, 2 * n1, tc), lambda b, j: (b, 0, j)),
        out_shape=jax.ShapeDtypeStruct((B, 2 * n1, n2 * wc), BF16),
        compiler_params=_cparams(("arbitrary", "arbitrary")),
        name="fnet_stage1",
    )(consts["f1"], zc.reshape(B, n1, n2 * wc))
    kt = 8
    out = pl.pallas_call(
        functools.partial(_fnet2_kernel, kt=kt, n2=n2, norm=float(1.0 / math.sqrt(S * CG_C))),
        grid=(B, n1 // kt),
        in_specs=[
            pl.BlockSpec((None, 2, kt, n2, wc), lambda b, j: (b, 0, j, 0, 0)),
            _resident((CG_C, 2 * CG_C)),
            pl.BlockSpec((kt, n2, 2 * n2), lambda b, j: (j, 0, 0)),
        ],
        out_specs=pl.BlockSpec((None, n2, kt * wc), lambda b, j: (b, 0, j)),
        out_shape=jax.ShapeDtypeStruct((B, n2, n1 * wc), BF16),
        compiler_params=_cparams(("arbitrary", "arbitrary")),
        name="fnet_stage2",
    )(a.reshape(B, 2, n1, n2, wc), consts["cs"], consts["g"])
    return out.reshape(B, S, wc)


def _gla_consts():
    ts = GLA_TS
    r = np.arange(ts)[:, None]
    c = np.arange(ts)[None, :]
    same = (r // GLA_CHUNK) == (c // GLA_CHUNK)
    as_bf16 = lambda m: jnp.asarray(m.astype(np.float32), dtype=BF16)
    return {"cum_f": as_bf16(same & (c <= r)), "cum_b": as_bf16(same & (c >= r))}


def _rope_tables(S):
    half = ROPE_B // 2
    inv = ROPE_BASE ** (-jnp.arange(half, dtype=F32) / half)
    ang = jnp.arange(S, dtype=F32)[:, None] * inv[None, :]
    cos, sin = jnp.cos(ang), jnp.sin(ang)
    cc = jnp.concatenate([cos, cos], axis=-1)
    ss = jnp.concatenate([sin, sin], axis=-1)
    z = lambda w: jnp.zeros((S, w), F32)
    return {
        "cosq": jnp.concatenate([jnp.ones((S, NOPE_B), F32), cc, z(HEAD_SLOT - NOPE_B - ROPE_B)], axis=-1),
        "sinq": jnp.concatenate([z(NOPE_B), ss, z(HEAD_SLOT - NOPE_B - ROPE_B)], axis=-1),
        "cosk": jnp.concatenate([cc, z(LANES - ROPE_B)], axis=-1),
        "sink": jnp.concatenate([ss, z(LANES - ROPE_B)], axis=-1),
    }


def _fnet_consts(S):
    n1 = 1 << (int(math.log2(S)) // 2)
    n2 = S // n1
    two_pi = 2.0 * math.pi
    k1 = jnp.arange(n1, dtype=jnp.int32)
    ph1 = ((k1[:, None] * k1[None, :]) % n1).astype(F32) * (two_pi / n1)
    f1 = jnp.concatenate([jnp.cos(ph1), -jnp.sin(ph1)], axis=0).astype(BF16)
    ch = jnp.arange(CG_C, dtype=jnp.int32)
    phc = ((ch[:, None] * ch[None, :]) % CG_C).astype(F32) * (two_pi / CG_C)
    cs = jnp.concatenate([jnp.cos(phc), jnp.sin(phc)], axis=-1).astype(BF16)
    s2 = jnp.arange(n2, dtype=jnp.int32)
    k = k1[:, None, None] + n1 * s2[None, :, None]
    ph = ((k * s2[None, None, :]) % S).astype(F32) * (two_pi / S)
    g = jnp.concatenate([jnp.cos(ph), jnp.sin(ph)], axis=-1).astype(BF16)
    return {"n1": n1, "n2": n2, "f1": f1, "cs": cs, "g": g}


def _even_weights(w_in, w_out, w_alpha, b_alpha, gla_gain, q_norm, w_q_b, kv_norm, w_kv_b):
    D = D_MODEL
    offs = np.cumsum([0, 256, 256, 512, 512, 2 * ALPHA_RANK, Q_RANK, KV_RANK, ROPE_B])
    col = lambda i: w_in[:, offs[i]:offs[i + 1]]
    kr = col(7)
    half = ROPE_B // 2
    kr_rot = jnp.concatenate([-kr[:, half:], kr[:, :half]], axis=-1)
    zc = lambda w: jnp.zeros((D, w), F32)
    win = jnp.concatenate([col(0), col(1), col(2), col(3), col(5), col(6),
                           kr, col(4), zc(64), kr_rot, zc(96)], axis=-1).astype(BF16)
    assert win.shape[1] == EV_WIDTH
    wal = jnp.zeros((LANES, 512), F32)
    wal = wal.at[ROPE_B:ROPE_B + ALPHA_RANK, :256].set(w_alpha[0])
    wal = wal.at[ROPE_B + ALPHA_RANK:ROPE_B + 2 * ALPHA_RANK, 256:].set(w_alpha[1])
    bal = jnp.concatenate([b_alpha[0], b_alpha[1]])[None, :]
    wq = w_q_b.reshape(Q_RANK, H_B, NOPE_B + ROPE_B)
    qn, qr = wq[..., :NOPE_B], wq[..., NOPE_B:]
    pad = HEAD_SLOT - NOPE_B - ROPE_B
    zq = lambda w: jnp.zeros((Q_RANK, H_B, w), F32)
    wq_main = jnp.concatenate([qn, qr, zq(pad)], axis=-1).reshape(Q_RANK, MLA_W)
    wq_rot = jnp.concatenate([zq(NOPE_B), -qr[..., half:], qr[..., :half], zq(pad)], axis=-1).reshape(Q_RANK, MLA_W)
    wq2 = jnp.concatenate([wq_main, wq_rot], axis=-1).astype(BF16)
    wkv = w_kv_b.reshape(KV_RANK, H_B, NOPE_B + V_B)
    zk = lambda w: jnp.zeros((KV_RANK, H_B, w), F32)
    wk = jnp.concatenate([wkv[..., :NOPE_B], zk(HEAD_SLOT - NOPE_B)], axis=-1).reshape(KV_RANK, MLA_W).astype(BF16)
    wv = jnp.concatenate([wkv[..., NOPE_B:], zk(HEAD_SLOT - V_B)], axis=-1).reshape(KV_RANK, MLA_W).astype(BF16)
    e = np.zeros((LANES, H_B, HEAD_SLOT), np.float32)
    for r in range(ROPE_B):
        e[r, :, NOPE_B + r] = 1.0
    e = jnp.asarray(e.reshape(LANES, MLA_W), dtype=BF16)
    wo_a = w_out[:H_A * DV_A].astype(BF16)
    wo_b = jnp.concatenate([w_out[H_A * DV_A:].reshape(H_B, V_B, D),
                            jnp.zeros((H_B, HEAD_SLOT - V_B, D), F32)], axis=1).reshape(MLA_W, D).astype(BF16)
    return {"w_in": win, "w_alpha": wal.astype(BF16), "b_alpha": bal, "q_norm": q_norm[None, :], "w_q": wq2,
            "kv_norm": kv_norm[None, :], "w_k": wk, "w_v": wv, "e_rope": e, "gain": gla_gain[None, :],
            "wo_a": wo_a, "wo_b": wo_b}


def kernel(x_prompt, x_sample, c_prompt, c_sample, ada_w, ada_b, norm_pre, norm_post, ffn_w13, ffn_w2,
           ev_w_in, ev_w_out, gla_w_alpha, gla_b_alpha, gla_norm, mla_q_norm, mla_w_q_b, mla_kv_norm,
           mla_w_kv_b, od_w_in, od_w_out, sgu_norm, sgu_w_s, sgu_b):
    groups = [(x_prompt, 0), (x_sample, x_prompt.shape[0])]
    c_all = jnp.concatenate([c_prompt, c_sample], axis=0)
    mod5 = _ada(c_all, ada_w, ada_b).reshape(DEPTH, c_all.shape[0], 3, 3, D_MODEL)

    w13 = ffn_w13.astype(BF16)
    w2 = ffn_w2.astype(BF16)
    gla_consts = _gla_consts()
    rope = {x.shape[1]: _rope_tables(x.shape[1]) for x, _ in groups}
    fnet = {x.shape[1]: _fnet_consts(x.shape[1]) for x, _ in groups}
    even_w = [_even_weights(ev_w_in[i], ev_w_out[i], gla_w_alpha[i], gla_b_alpha[i], gla_norm[i], mla_q_norm[i],
                            mla_w_q_b[i], mla_kv_norm[i], mla_w_kv_b[i]) for i in range(ev_w_in.shape[0])]
    od_win = od_w_in.astype(BF16)
    od_wout = od_w_out.astype(BF16)
    sgu_ws = sgu_w_s.astype(BF16)
    sgu_bias = jnp.broadcast_to(sgu_b[..., None], sgu_b.shape + (DG_D,))

    outs = []
    for x, b_off in groups:
        S = x.shape[1]
        for l in range(DEPTH):
            pre = lambda s: norm_pre[l, s][None, :]
            post = lambda s: norm_post[l, s][None, :]
            ffn0 = _ffn_stage_spec(mod5, l, 0, 0, b_off, pre(0), post(0), w13, w2)
            ffn2 = _ffn_stage_spec(mod5, l, 2, 1, b_off, pre(2), post(2), w13, w2)
            i = l // 2
            if l % 2 == 0:
                wp = even_w[i]
                (x,) = _chain(x, [ffn0], FFN_TM, "ffn")
                qg, kg, vg, g, lg, q, k, vt = _chain(
                    x, [_even_in_stage_spec(mod5, l, b_off, pre(1), wp, rope[S])], CHAIN_TM, "even_in", write_x=False)
                o_f, o_b = _gla(qg, kg, vg, lg, gla_consts)
                ao = _attn(q, k, vt)
                (x,) = _chain(x, [_even_out_stage_spec(mod5, l, b_off, post(1), o_f, o_b, g, ao, wp), ffn2],
                              CHAIN_TM, "even_out_ffn")
            else:
                (x,) = _chain(x, [ffn0], FFN_TM, "ffn")
                zc, od = _chain(
                    x, [_odd_in_stage_spec(mod5, l, b_off, pre(1), od_win[i], sgu_norm[i][None, :], sgu_ws[i],
                                           sgu_bias[i])], CHAIN_TM, "odd_in", write_x=False)
                fc = _fnet(zc, fnet[S])
                (x,) = _chain(x, [_odd_out_stage_spec(mod5, l, b_off, post(1), fc, od, od_wout[i, :W_C],
                                                      od_wout[i, W_C:]), ffn2], CHAIN_TM, "odd_out_ffn")
        outs.append(x)
    return tuple(outs)
```
